```python
import math
import jax, jax.numpy as jnp
from jax import lax
import numpy as np

D_MODEL = 1024
BATCH = 16
SEQ = 2048
DEPTH = 4

N_MIXERS = 4
EPS = 1e-6
POOL_WINDOWS = (2, 4, 8, 16)
POOL_GROUP = D_MODEL // len(POOL_WINDOWS)
S5_GROUP = 16
S5_GROUPS = D_MODEL // S5_GROUP
S5_STATE = 64
S5_DT_MIN = 1e-3
S5_DT_MAX = 1e-1
LRU_WIDTH = D_MODEL
LRU_BLOCKS = 4
LRU_BLOCK = LRU_WIDTH // LRU_BLOCKS
LRU_CONV = 4
LRU_C = 8.0
SB_HEADS = 16
SB_HEAD_DIM = D_MODEL // SB_HEADS
SB_Q_BLOCK = 128
FFN_HIDDEN = 2816
FFN_CONV = 3

kernel_name = "interleaved_pool_s5_rglru_stickbreak_trunk"


def n_layers_of(m):
    return len(range(m, DEPTH, N_MIXERS))


def rms_norm(x, g):
    xf = x.astype(jnp.float32)
    y = xf * lax.rsqrt(jnp.mean(xf * xf, axis=-1, keepdims=True) + EPS)
    return (y * g.astype(jnp.float32)).astype(x.dtype)


def causal_depthwise_conv(x, w, b):
    k_width = w.shape[0]
    seq = x.shape[1]
    xp = jnp.pad(x, ((0, 0), (k_width - 1, 0), (0, 0)))
    y = b
    for k in range(k_width):
        y = y + w[k] * xp[:, k:k + seq]
    return y


def linear_scan_combine(left, right):
    a_l, b_l = left
    a_r, b_r = right
    return a_r * a_l, a_r * b_l + b_r


def pool_mixer(x, w, b, scale):
    bsz, seq, _ = x.shape
    xf = x.astype(jnp.float32)
    cs = jnp.pad(jnp.cumsum(xf, axis=1), ((0, 0), (1, 0), (0, 0)))
    pos = jnp.arange(seq)
    groups = []
    for gi, w_len in enumerate(POOL_WINDOWS):
        c = cs[..., gi * POOL_GROUP:(gi + 1) * POOL_GROUP]
        lo = jnp.maximum(pos + 1 - w_len, 0)
        window_sum = c[:, 1:] - jnp.take(c, lo, axis=1)
        count = (pos + 1 - lo).astype(jnp.float32)[:, None]
        groups.append(window_sum / count - xf[..., gi * POOL_GROUP:(gi + 1) * POOL_GROUP])
    d = jnp.stack(groups, axis=2)
    y = jnp.einsum('bsgc,gcd->bsgd', d, w.astype(jnp.float32)).reshape(bsz, seq, D_MODEL) + b
    return (scale * y).astype(x.dtype)


def s5_mixer(x, lam_re, lam_im, log_dt, b_re, b_im, c_re, c_im, d_skip, w_out, b_out):
    f32 = jnp.float32
    bsz, seq, _ = x.shape
    xf = x.astype(f32)
    u = xf.reshape(bsz, seq, S5_GROUPS, S5_GROUP)
    lam = lax.complex(jnp.minimum(lam_re.astype(f32), -1e-4), lam_im.astype(f32))
    dt = jnp.exp(log_dt.astype(f32))[:, None]
    lam_bar = jnp.exp(lam * dt)
    b_bar = ((lam_bar - 1.0) / lam)[..., None] * lax.complex(b_re.astype(f32), b_im.astype(f32))
    bu = lax.complex(jnp.einsum('bsgh,gph->bsgp', u, jnp.real(b_bar)),
                     jnp.einsum('bsgh,gph->bsgp', u, jnp.imag(b_bar)))
    a = jnp.broadcast_to(lam_bar, (seq,) + lam_bar.shape)[None]
    _, states = lax.associative_scan(linear_scan_combine, (a, bu), axis=1)
    y = (jnp.einsum('bsgp,ghp->bsgh', jnp.real(states), c_re.astype(f32))
         - jnp.einsum('bsgp,ghp->bsgh', jnp.imag(states), c_im.astype(f32)))
    y = y.reshape(bsz, seq, D_MODEL) + d_skip.astype(f32) * xf
    y = jax.nn.gelu(y).astype(x.dtype)
    val, gate = jnp.split(y @ w_out + b_out, 2, axis=-1)
    return val * jax.nn.sigmoid(gate)


def rglru_mixer(x, w_in, conv_w, conv_b, w_a, b_a, w_x, b_x, lam, w_out):
    f32 = jnp.float32
    bsz, seq, _ = x.shape
    gate_branch, rec = jnp.split(x @ w_in, 2, axis=-1)
    rec = causal_depthwise_conv(rec, conv_w, conv_b).astype(f32)
    rb = rec.reshape(bsz, seq, LRU_BLOCKS, LRU_BLOCK)
    r = jax.nn.sigmoid(jnp.einsum('bsnc,ncd->bsnd', rb, w_a.astype(f32)).reshape(bsz, seq, LRU_WIDTH) + b_a)
    i = jax.nn.sigmoid(jnp.einsum('bsnc,ncd->bsnd', rb, w_x.astype(f32)).reshape(bsz, seq, LRU_WIDTH) + b_x)
    log_a = -LRU_C * r * jax.nn.softplus(-lam.astype(f32))
    a = jnp.exp(log_a)
    mult = jnp.sqrt(-jnp.expm1(2.0 * log_a))
    _, h = lax.associative_scan(linear_scan_combine, (a, mult * (i * rec)), axis=1)
    y = jax.nn.gelu(gate_branch.astype(f32)) * h
    return y.astype(x.dtype) @ w_out


def stick_breaking_mixer(x, w_qkv, q_g, k_g, w_o):
    bsz, seq, _ = x.shape
    q, k, v = jnp.split(x @ w_qkv, 3, axis=-1)
    to_heads = lambda t: t.reshape(bsz, seq, SB_HEADS, SB_HEAD_DIM).transpose(0, 2, 1, 3)
    q = rms_norm(to_heads(q), q_g)
    k = rms_norm(to_heads(k), k_g)
    v = to_heads(v)
    scale = 1.0 / math.sqrt(SB_HEAD_DIM)
    outs = []
    for start in range(0, seq, SB_Q_BLOCK):
        end = start + SB_Q_BLOCK
        kb, vb = k[:, :, :end], v[:, :, :end]
        z = jnp.einsum('bhtd,bhsd->bhts', q[:, :, start:end], kb).astype(jnp.float32) * scale
        t_pos = start + jnp.arange(SB_Q_BLOCK)[:, None]
        s_pos = jnp.arange(end)[None, :]
        mask = s_pos < t_pos
        log_1m_beta = jnp.where(mask, jax.nn.log_sigmoid(-z), 0.0)
        rest = lax.cumsum(log_1m_beta, axis=3, reverse=True) - log_1m_beta
        att = jnp.where(mask, jnp.exp(jax.nn.log_sigmoid(z) + rest), 0.0)
        outs.append(jnp.einsum('bhts,bhsd->bhtd', att.astype(v.dtype), vb))
    o = jnp.concatenate(outs, axis=2).transpose(0, 2, 1, 3).reshape(bsz, seq, D_MODEL)
    return o @ w_o


def conv_ffn(x, w_in, conv_w, conv_b, w_out):
    h = causal_depthwise_conv(x @ w_in, conv_w, conv_b)
    val, gate = jnp.split(h, 2, axis=-1)
    return (jax.nn.silu(gate) * val) @ w_out


def setup_inputs(seed: int = 0) -> dict:
    key = jax.random.key(seed)
    ks = iter(jax.random.split(key, 40))
    nrm = lambda shape, std: jax.random.normal(next(ks), shape, jnp.float32) * std
    gain = lambda shape: 1.0 + nrm(shape, 0.02)
    nA, nB, nC, nD = (n_layers_of(m) for m in range(N_MIXERS))
    G, P, H = S5_GROUPS, S5_STATE, S5_GROUP
    a0 = jax.random.uniform(next(ks), (nC, LRU_WIDTH), jnp.float32, 0.9, 0.999)
    return {
        "x": nrm((BATCH, SEQ, D_MODEL), 1.0),
        "norm_mix_g": gain((DEPTH, D_MODEL)),
        "norm_ffn_g": gain((DEPTH, D_MODEL)),
        "pool_w": nrm((nA, len(POOL_WINDOWS), POOL_GROUP, POOL_GROUP), POOL_GROUP ** -0.5),
        "pool_b": nrm((nA, D_MODEL), 0.01),
        "pool_scale": 1.0 + nrm((nA, D_MODEL), 0.1),
        "s5_lam_re": -0.5 + nrm((nA * 0 + nB, G, P), 0.01),
        "s5_lam_im": math.pi * jnp.arange(P, dtype=jnp.float32) + nrm((nB, G, P), 0.01),
        "s5_log_dt": jax.random.uniform(next(ks), (nB, G), jnp.float32, math.log(S5_DT_MIN), math.log(S5_DT_MAX)),
        "s5_b_re": nrm((nB, G, P, H), (2 * H) ** -0.5),
        "s5_b_im": nrm((nB, G, P, H), (2 * H) ** -0.5),
        "s5_c_re": nrm((nB, G, H, P), P ** -0.5),
        "s5_c_im": nrm((nB, G, H, P), P ** -0.5),
        "s5_d": nrm((nB, D_MODEL), 1.0),
        "s5_w_out": nrm((nB, D_MODEL, 2 * D_MODEL), D_MODEL ** -0.5),
        "s5_b_out": nrm((nB, 2 * D_MODEL), 0.01),
        "lru_w_in": nrm((nC, D_MODEL, 2 * LRU_WIDTH), D_MODEL ** -0.5),
        "lru_conv_w": nrm((nC, LRU_CONV, LRU_WIDTH), LRU_CONV ** -0.5),
        "lru_conv_b": nrm((nC, LRU_WIDTH), 0.01),
        "lru_w_a": nrm((nC, LRU_BLOCKS, LRU_BLOCK, LRU_BLOCK), LRU_BLOCK ** -0.5),
        "lru_b_a": nrm((nC, LRU_WIDTH), 0.01),
        "lru_w_x": nrm((nC, LRU_BLOCKS, LRU_BLOCK, LRU_BLOCK), LRU_BLOCK ** -0.5),
        "lru_b_x": nrm((nC, LRU_WIDTH), 0.01),
        "lru_lam": jnp.log(a0) - jnp.log1p(-a0),
        "lru_w_out": nrm((nC, LRU_WIDTH, D_MODEL), LRU_WIDTH ** -0.5),
        "sb_w_qkv": nrm((nD, D_MODEL, 3 * D_MODEL), D_MODEL ** -0.5),
        "sb_q_g": gain((nD, SB_HEAD_DIM)),
        "sb_k_g": gain((nD, SB_HEAD_DIM)),
        "sb_w_o": nrm((nD, D_MODEL, D_MODEL), D_MODEL ** -0.5),
        "ffn_w_in": nrm((DEPTH, D_MODEL, 2 * FFN_HIDDEN), D_MODEL ** -0.5),
        "ffn_conv_w": nrm((DEPTH, FFN_CONV, 2 * FFN_HIDDEN), FFN_CONV ** -0.5),
        "ffn_conv_b": nrm((DEPTH, 2 * FFN_HIDDEN), 0.01),
        "ffn_w_out": nrm((DEPTH, FFN_HIDDEN, D_MODEL), FFN_HIDDEN ** -0.5),
    }


def reference(x, norm_mix_g, norm_ffn_g,
              pool_w, pool_b, pool_scale,
              s5_lam_re, s5_lam_im, s5_log_dt, s5_b_re, s5_b_im, s5_c_re, s5_c_im, s5_d, s5_w_out, s5_b_out,
              lru_w_in, lru_conv_w, lru_conv_b, lru_w_a, lru_b_a, lru_w_x, lru_b_x, lru_lam, lru_w_out,
              sb_w_qkv, sb_q_g, sb_k_g, sb_w_o,
              ffn_w_in, ffn_conv_w, ffn_conv_b, ffn_w_out):
    for layer in range(DEPTH):
        m, j = layer % N_MIXERS, layer // N_MIXERS
        h = rms_norm(x, norm_mix_g[layer])
        if m == 0:
            y = pool_mixer(h, pool_w[j], pool_b[j], pool_scale[j])
        elif m == 1:
            y = s5_mixer(h, s5_lam_re[j], s5_lam_im[j], s5_log_dt[j], s5_b_re[j], s5_b_im[j],
                         s5_c_re[j], s5_c_im[j], s5_d[j], s5_w_out[j], s5_b_out[j])
        elif m == 2:
            y = rglru_mixer(h, lru_w_in[j], lru_conv_w[j], lru_conv_b[j], lru_w_a[j], lru_b_a[j],
                            lru_w_x[j], lru_b_x[j], lru_lam[j], lru_w_out[j])
        else:
            y = stick_breaking_mixer(h, sb_w_qkv[j], sb_q_g[j], sb_k_g[j], sb_w_o[j])
        x = x + y.astype(x.dtype)
        f = conv_ffn(rms_norm(x, norm_ffn_g[layer]), ffn_w_in[layer], ffn_conv_w[layer], ffn_conv_b[layer], ffn_w_out[layer])
        x = x + f.astype(x.dtype)
    return x
```

```python
import functools
import math

import jax
import jax.numpy as jnp
from jax import lax
from jax.experimental import pallas as pl
from jax.experimental.pallas import tpu as pltpu

F32 = jnp.float32
BF16 = jnp.bfloat16

EPS = 1e-6
HALO = 16
POOL_WINDOWS = (2, 4, 8, 16)
S5_GROUP = 16
S5_STATE = 64
S5_SLAB = 256
LRU_BLOCKS = 4
LRU_CONV = 4
LRU_C = 8.0
SB_HEAD_DIM = 64
FFN_CONV = 3
FFN_CHUNK = 256
VMEM_LIMIT = 56 * 1024 * 1024


def _cparams():
    return pltpu.CompilerParams(
        dimension_semantics=("arbitrary", "arbitrary"),
        vmem_limit_bytes=VMEM_LIMIT)


def _rms(x, g):
    ms = jnp.mean(x * x, axis=-1, keepdims=True)
    return x * lax.rsqrt(ms + EPS) * g


def _gelu(x):
    c = math.sqrt(2.0 / math.pi)
    return 0.5 * x * (1.0 + jnp.tanh(c * (x + 0.044715 * (x * x * x))))


def _sigmoid(x):
    return 1.0 / (1.0 + jnp.exp(-x))


def _softplus(x):
    return jnp.maximum(x, 0.0) + jnp.log(1.0 + jnp.exp(-jnp.abs(x)))


def _dot(a, b):
    return jnp.dot(a, b, preferred_element_type=F32)


def _split(x):
    hi = x.astype(BF16)
    lo = (x - hi.astype(F32)).astype(BF16)
    return hi, lo


def _normed_with_halo(xh_ref, x_ref, g, xn_ref):
    first = pl.program_id(1) == 0
    hn = _rms(xh_ref[0], g)
    xn_ref[0:HALO, :] = jnp.where(first, 0.0, hn).astype(xn_ref.dtype)
    xn_ref[HALO:, :] = _rms(x_ref[0], g).astype(xn_ref.dtype)


def _tile_specs(ts, d):
    per = ts // HALO
    halo = pl.BlockSpec((1, HALO, d), lambda b, s: (b, jnp.maximum(s * per - 1, 0), 0))
    tile = pl.BlockSpec((1, ts, d), lambda b, s: (b, s, 0))
    return halo, tile


def _const_spec(shape):
    zeros = (0,) * len(shape)
    return pl.BlockSpec(shape, lambda b, s: zeros, pipeline_mode=pl.Buffered(1))


def _ffn_kernel(xh_ref, x_ref, g_ref, win_ref, cw_ref, cb_ref, wout_ref, o_ref,
                xn_ref, h_ref, a_ref, *, ts, hidden):
    _normed_with_halo(xh_ref, x_ref, g_ref[...], xn_ref)
    for c in range(hidden // FFN_CHUNK):
        conv = []
        for part in range(2):
            col = part * hidden + c * FFN_CHUNK
            h_ref[part] = _dot(xn_ref[...], win_ref[:, col:col + FFN_CHUNK])
            y = cb_ref[:, col:col + FFN_CHUNK]
            for k in range(FFN_CONV):
                off = HALO - (FFN_CONV - 1) + k
                y = y + cw_ref[k:k + 1, col:col + FFN_CHUNK] * h_ref[part, off:off + ts, :]
            conv.append(y)
        val, gate = conv
        a_ref[:, c * FFN_CHUNK:(c + 1) * FFN_CHUNK] = (gate * _sigmoid(gate) * val).astype(BF16)
    o_ref[0] = x_ref[0] + _dot(a_ref[...], wout_ref[...])


def _conv_ffn(x, g, w_in, conv_w, conv_b, w_out, *, ts):
    bsz, seq, d = x.shape
    hidden = w_out.shape[0]
    halo, tile = _tile_specs(ts, d)
    return pl.pallas_call(
        functools.partial(_ffn_kernel, ts=ts, hidden=hidden),
        grid=(bsz, seq // ts),
        in_specs=[halo, tile, _const_spec((1, d)), _const_spec((d, 2 * hidden)),
                  _const_spec((FFN_CONV, 2 * hidden)), _const_spec((1, 2 * hidden)),
                  _const_spec((hidden, d))],
        out_specs=tile,
        out_shape=jax.ShapeDtypeStruct(x.shape, x.dtype),
        scratch_shapes=[pltpu.VMEM((HALO + ts, d), BF16),
                        pltpu.VMEM((2, HALO + ts, FFN_CHUNK), F32),
                        pltpu.VMEM((ts, hidden), BF16)],
        compiler_params=_cparams(),
    )(x, x, g.reshape(1, d), w_in.astype(BF16), conv_w, conv_b.reshape(1, -1), w_out.astype(BF16))


def _pool_kernel(xh_ref, x_ref, g_ref, w_ref, b_ref, sc_ref, o_ref, hn_ref, *, ts, group):
    _normed_with_halo(xh_ref, x_ref, g_ref[...], hn_ref)
    pos = pl.program_id(1) * ts + lax.broadcasted_iota(jnp.int32, (ts, 1), 0)
    for gi, win in enumerate(POOL_WINDOWS):
        lanes = slice(gi * group, (gi + 1) * group)
        h = hn_ref[HALO:HALO + ts, lanes]
        acc = h
        for k in range(1, win):
            acc = acc + hn_ref[HALO - k:HALO - k + ts, lanes]
        count = jnp.minimum(pos + 1, win).astype(F32)
        diff = acc / count - h
        y = _dot(diff.astype(BF16), w_ref[gi]) + b_ref[:, lanes]
        o_ref[0, :, lanes] = x_ref[0, :, lanes] + sc_ref[:, lanes] * y


def _pool_layer(x, g, w, b, scale, *, ts):
    bsz, seq, d = x.shape
    ngroup, group = w.shape[0], w.shape[1]
    halo, tile = _tile_specs(ts, d)
    return pl.pallas_call(
        functools.partial(_pool_kernel, ts=ts, group=group),
        grid=(bsz, seq // ts),
        in_specs=[halo, tile, _const_spec((1, d)), _const_spec((ngroup, group, group)),
                  _const_spec((1, d)), _const_spec((1, d))],
        out_specs=tile,
        out_shape=jax.ShapeDtypeStruct(x.shape, x.dtype),
        scratch_shapes=[pltpu.VMEM((HALO + ts, d), F32)],
        compiler_params=_cparams(),
    )(x, x, g.reshape(1, d), w.astype(BF16), b.reshape(1, d), scale.reshape(1, d))


S5_LANE_CHUNK = 256


def _s5_kernel(x_ref, g_ref, wbr_ref, wbi_ref, pw_ref, wcr_ref, wci_ref, dsk_ref, wout_ref, bout_ref,
               o_ref, u_ref, sre_ref, sim_ref, cre_ref, cim_ref, y_ref, *, ts, d):
    nslab = d // S5_SLAB
    width = sre_ref.shape[1] // nslab
    u = _rms(x_ref[0], g_ref[...])
    u_ref[...] = u
    ub = u.astype(BF16)
    for j in range(nslab):
        us = ub[:, j * S5_SLAB:(j + 1) * S5_SLAB]
        sre_ref[:, j * width:(j + 1) * width] = _dot(us, wbr_ref[j])
        sim_ref[:, j * width:(j + 1) * width] = _dot(us, wbi_ref[j])

    @pl.when(pl.program_id(1) == 0)
    def _():
        cre_ref[...] = jnp.zeros_like(cre_ref)
        cim_ref[...] = jnp.zeros_like(cim_ref)

    lc = S5_LANE_CHUNK
    for c in range(sre_ref.shape[1] // lc):
        lanes = slice(c * lc, (c + 1) * lc)
        pws = [(pw_ref[i, 0, :, lanes], pw_ref[i, 1, :, lanes]) for i in range(4)]

        def body(i, carry, lanes=lanes, pws=pws):
            c_re, c_im = carry
            r0 = pl.multiple_of(i * 8, 8)
            b_re = sre_ref[pl.ds(r0, 8), lanes]
            b_im = sim_ref[pl.ds(r0, 8), lanes]
            for step, shift in enumerate((1, 2, 4)):
                p_re, p_im = pws[step]
                s_re = pltpu.roll(b_re, shift, 0)
                s_im = pltpu.roll(b_im, shift, 0)
                b_re, b_im = (b_re + (p_re * s_re - p_im * s_im),
                              b_im + (p_re * s_im + p_im * s_re))
            p_re, p_im = pws[3]
            h_re = b_re + (p_re * c_re - p_im * c_im)
            h_im = b_im + (p_re * c_im + p_im * c_re)
            sre_ref[pl.ds(r0, 8), lanes] = h_re
            sim_ref[pl.ds(r0, 8), lanes] = h_im
            return (jnp.broadcast_to(h_re[7:8, :], (8, lc)), jnp.broadcast_to(h_im[7:8, :], (8, lc)))

        c_re, c_im = lax.fori_loop(0, ts // 8, body, (cre_ref[:, lanes], cim_ref[:, lanes]), unroll=2)
        cre_ref[:, lanes] = c_re
        cim_ref[:, lanes] = c_im

    for j in range(nslab):
        cols = slice(j * width, (j + 1) * width)
        lanes = slice(j * S5_SLAB, (j + 1) * S5_SLAB)
        y = (_dot(sre_ref[:, cols].astype(BF16), wcr_ref[j])
             + _dot(sim_ref[:, cols].astype(BF16), wci_ref[j]))
        y = y + dsk_ref[:, lanes] * u_ref[:, lanes]
        y_ref[:, lanes] = _gelu(y).astype(BF16)
    z = _dot(y_ref[...], wout_ref[...]) + bout_ref[...]
    o_ref[0] = x_ref[0] + z[:, :d] * _sigmoid(z[:, d:])


def _s5_tables(lam_re, lam_im, log_dt, b_re, b_im, c_re, c_im, d):
    ngroups, nstate = lam_re.shape
    per_slab = S5_SLAB // S5_GROUP
    nslab = d // S5_SLAB
    lam = lax.complex(jnp.minimum(lam_re.astype(F32), -1e-4), lam_im.astype(F32))
    dt = jnp.exp(log_dt.astype(F32))[:, None]
    lam_bar = jnp.exp(lam * dt)
    b_bar = ((lam_bar - 1.0) / lam)[..., None] * lax.complex(b_re.astype(F32), b_im.astype(F32))
    eye = jnp.eye(per_slab, dtype=F32)

    def b_slabs(t):
        t = t.reshape(nslab, per_slab, nstate, S5_GROUP)
        return jnp.einsum('jgph,gk->jghkp', t, eye).reshape(nslab, S5_SLAB, per_slab * nstate)

    def c_slabs(t):
        t = t.reshape(nslab, per_slab, S5_GROUP, nstate)
        return jnp.einsum('jghp,gk->jgpkh', t, eye).reshape(nslab, per_slab * nstate, S5_SLAB)

    wbr = b_slabs(jnp.real(b_bar)).astype(BF16)
    wbi = b_slabs(jnp.imag(b_bar)).astype(BF16)
    wcr = c_slabs(c_re.astype(F32)).astype(BF16)
    wci = c_slabs(-c_im.astype(F32)).astype(BF16)

    lam1 = lam_bar.reshape(1, -1)
    lam2 = lam1 * lam1
    lam4 = lam2 * lam2
    row = jnp.arange(8)[:, None]
    steps = [jnp.where(row >= s, p, 0.0) for s, p in ((1, lam1), (2, lam2), (4, lam4))]
    ramp = [lam1]
    for _ in range(7):
        ramp.append(ramp[-1] * lam1)
    steps.append(jnp.concatenate(ramp, axis=0))
    pw = jnp.stack(steps)
    pw = jnp.stack([jnp.real(pw), jnp.imag(pw)], axis=1).astype(F32)
    return wbr, wbi, pw, wcr, wci


def _s5_layer(x, g, lam_re, lam_im, log_dt, b_re, b_im, c_re, c_im, d_skip, w_out, b_out, *, ts):
    bsz, seq, d = x.shape
    wbr, wbi, pw, wcr, wci = _s5_tables(lam_re, lam_im, log_dt, b_re, b_im, c_re, c_im, d)
    nstate_cols = pw.shape[-1]
    tile = pl.BlockSpec((1, ts, d), lambda b, s: (b, s, 0))
    return pl.pallas_call(
        functools.partial(_s5_kernel, ts=ts, d=d),
        grid=(bsz, seq // ts),
        in_specs=[tile, _const_spec((1, d)), _const_spec(wbr.shape), _const_spec(wbi.shape),
                  _const_spec(pw.shape), _const_spec(wcr.shape), _const_spec(wci.shape),
                  _const_spec((1, d)), _const_spec((d, 2 * d)), _const_spec((1, 2 * d))],
        out_specs=tile,
        out_shape=jax.ShapeDtypeStruct(x.shape, x.dtype),
        scratch_shapes=[pltpu.VMEM((ts, d), F32),
                        pltpu.VMEM((ts, nstate_cols), F32), pltpu.VMEM((ts, nstate_cols), F32),
                        pltpu.VMEM((8, nstate_cols), F32), pltpu.VMEM((8, nstate_cols), F32),
                        pltpu.VMEM((ts, d), BF16)],
        compiler_params=_cparams(),
    )(x, g.reshape(1, d), wbr, wbi, pw, wcr, wci, d_skip.reshape(1, d), w_out.astype(BF16),
      b_out.reshape(1, 2 * d))


def _lru_kernel(xh_ref, x_ref, g_ref, win_ref, cw_ref, cb_ref, wa_ref, ba_ref, wx_ref, bx_ref, lam_ref,
                wout_ref, o_ref, xn_ref, gb_ref, rec_ref, a_ref, b_ref, carry_ref, *, ts, width):
    _normed_with_halo(xh_ref, x_ref, g_ref[...], xn_ref)
    gb_ref[...] = _dot(xn_ref[HALO:, :], win_ref[:, :width])
    rec_ref[...] = _dot(xn_ref[...], win_ref[:, width:])
    block = width // LRU_BLOCKS
    for n in range(LRU_BLOCKS):
        lanes = slice(n * block, (n + 1) * block)
        rec = cb_ref[:, lanes]
        for k in range(LRU_CONV):
            off = HALO - (LRU_CONV - 1) + k
            rec = rec + cw_ref[k:k + 1, lanes] * rec_ref[off:off + ts, lanes]
        rb = rec.astype(BF16)
        r = _sigmoid(_dot(rb, wa_ref[n]) + ba_ref[:, lanes])
        i = _sigmoid(_dot(rb, wx_ref[n]) + bx_ref[:, lanes])
        log_a = (-LRU_C * r) * _softplus(-lam_ref[:, lanes])
        a_ref[:, lanes] = jnp.exp(log_a)
        th = jnp.tanh(log_a)
        b_ref[:, lanes] = jnp.sqrt(-2.0 * th / (1.0 - th)) * (i * rec)

    @pl.when(pl.program_id(1) == 0)
    def _():
        carry_ref[...] = jnp.zeros_like(carry_ref)

    row = lax.broadcasted_iota(jnp.int32, (8, width), 0)

    def body(i, carry):
        r0 = pl.multiple_of(i * 8, 8)
        a = a_ref[pl.ds(r0, 8), :]
        b = b_ref[pl.ds(r0, 8), :]
        for shift in (1, 2, 4):
            has_partner = row >= shift
            a_prev = jnp.where(has_partner, pltpu.roll(a, shift, 0), 1.0)
            b_prev = jnp.where(has_partner, pltpu.roll(b, shift, 0), 0.0)
            b = a * b_prev + b
            a = a * a_prev
        h = a * carry + b
        b_ref[pl.ds(r0, 8), :] = h
        return jnp.broadcast_to(h[7:8, :], (8, width))

    carry_ref[...] = lax.fori_loop(0, ts // 8, body, carry_ref[...], unroll=2)
    y = (_gelu(gb_ref[...]) * b_ref[...]).astype(BF16)
    o_ref[0] = x_ref[0] + _dot(y, wout_ref[...])


def _lru_layer(x, g, w_in, conv_w, conv_b, w_a, b_a, w_x, b_x, lam, w_out, *, ts):
    bsz, seq, d = x.shape
    width = w_out.shape[0]
    halo, tile = _tile_specs(ts, d)
    row = lambda v: v.reshape(1, -1)
    return pl.pallas_call(
        functools.partial(_lru_kernel, ts=ts, width=width),
        grid=(bsz, seq // ts),
        in_specs=[halo, tile, _const_spec((1, d)), _const_spec((d, 2 * width)),
                  _const_spec((LRU_CONV, width)), _const_spec((1, width)),
                  _const_spec(w_a.shape), _const_spec((1, width)),
                  _const_spec(w_x.shape), _const_spec((1, width)), _const_spec((1, width)),
                  _const_spec((width, d))],
        out_specs=tile,
        out_shape=jax.ShapeDtypeStruct(x.shape, x.dtype),
        scratch_shapes=[pltpu.VMEM((HALO + ts, d), BF16),
                        pltpu.VMEM((ts, width), F32),
                        pltpu.VMEM((HALO + ts, width), F32),
                        pltpu.VMEM((ts, width), F32), pltpu.VMEM((ts, width), F32),
                        pltpu.VMEM((8, width), F32)],
        compiler_params=_cparams(),
    )(x, x, row(g), w_in.astype(BF16), conv_w, row(conv_b), w_a.astype(BF16), row(b_a),
      w_x.astype(BF16), row(b_x), row(lam), w_out.astype(BF16))


def _qkv_kernel(x_ref, g_ref, w_ref, qg_ref, kg_ref, gsum_ref, gexp_ref, q_ref, k_ref, v_ref, *, d):
    xn = _rms(x_ref[0], g_ref[...]).astype(BF16)

    def head_normed(t, gain):
        hi, lo = _split(t * t)
        ms = _dot(hi, gsum_ref[...]) + _dot(lo, gsum_ref[...])
        hi, lo = _split(lax.rsqrt(ms + EPS))
        return t * (_dot(hi, gexp_ref[...]) + _dot(lo, gexp_ref[...])) * gain

    scale = 1.0 / math.sqrt(SB_HEAD_DIM)
    q_ref[0] = (head_normed(_dot(xn, w_ref[:, :d]), qg_ref[...]) * scale).astype(BF16)
    k_ref[0] = head_normed(_dot(xn, w_ref[:, d:2 * d]), kg_ref[...]).astype(BF16)
    v_ref[0] = _dot(xn, w_ref[:, 2 * d:]).astype(BF16)


def _qkv_proj(x, g, w_qkv, q_g, k_g, *, ts):
    bsz, seq, d = x.shape
    heads = d // SB_HEAD_DIM
    head_of_lane = jnp.arange(d) // SB_HEAD_DIM
    onehot = (head_of_lane[:, None] == jnp.arange(128)[None, :])
    gsum = (onehot.astype(F32) / SB_HEAD_DIM).astype(BF16)
    gexp = onehot.T.astype(BF16)
    tile = pl.BlockSpec((1, ts, d), lambda b, s: (b, s, 0))
    out = jax.ShapeDtypeStruct(x.shape, BF16)
    return pl.pallas_call(
        functools.partial(_qkv_kernel, d=d),
        grid=(bsz, seq // ts),
        in_specs=[tile, _const_spec((1, d)), _const_spec((d, 3 * d)), _const_spec((1, d)),
                  _const_spec((1, d)), _const_spec((d, 128)), _const_spec((128, d))],
        out_specs=[tile, tile, tile],
        out_shape=[out, out, out],
        compiler_params=_cparams(),
    )(x, g.reshape(1, d), w_qkv.astype(BF16), jnp.tile(q_g, heads).reshape(1, d),
      jnp.tile(k_g, heads).reshape(1, d), gsum, gexp)


SB_TILE = 128


def _attn_kernel(q_ref, k_ref, v_ref, tri_ref, o_ref, acc_ref, run_ref):
    tq = SB_TILE
    qi = pl.program_id(2)
    q = q_ref[0]
    lane = lax.broadcasted_iota(jnp.int32, (tq, 2 * SB_HEAD_DIM), 1)
    first_head = lane < SB_HEAD_DIM
    q_heads = (jnp.where(first_head, q, jnp.zeros_like(q)), jnp.where(first_head, jnp.zeros_like(q), q))
    t_pos = lax.broadcasted_iota(jnp.int32, (tq, tq), 0)
    s_pos = lax.broadcasted_iota(jnp.int32, (tq, tq), 1)
    causal = s_pos < t_pos

    def block(kb, diagonal):
        k0 = pl.multiple_of(kb * tq, tq)
        kblk = k_ref[0, pl.ds(k0, tq), :]
        vblk = v_ref[0, pl.ds(k0, tq), :]
        for h in range(2):
            z = lax.dot_general(q_heads[h], kblk, (((1,), (1,)), ((), ())), preferred_element_type=F32)
            lsm = -_softplus(z)
            if diagonal:
                lsm = jnp.where(causal, lsm, 0.0)
            hi, lo = _split(lsm)
            sums = _dot(jnp.concatenate([hi, lo], axis=1), tri_ref[...])
            att = jnp.exp(z + lsm + sums[:, :tq] + run_ref[h])
            if diagonal:
                att = jnp.where(causal, att, 0.0)
            acc_ref[h] += _dot(att.astype(BF16), vblk)
            run_ref[h] += sums[:, tq:]

    acc_ref[...] = jnp.zeros_like(acc_ref)
    run_ref[...] = jnp.zeros_like(run_ref)
    block(qi, True)

    def earlier(j, carry):
        block(qi - 1 - j, False)
        return carry

    lax.fori_loop(0, qi, earlier, 0)
    o_ref[0] = jnp.where(first_head, acc_ref[0], acc_ref[1]).astype(BF16)


def _attention(q, k, v):
    bsz, seq, d = q.shape
    tq = SB_TILE
    lanes = 2 * SB_HEAD_DIM
    j = jnp.arange(2 * tq)[:, None] % tq
    s = jnp.arange(2 * tq)[None, :]
    tri = ((s >= tq) | (j > s)).astype(BF16)
    qspec = pl.BlockSpec((1, tq, lanes), lambda b, h, i: (b, i, h))
    kvspec = pl.BlockSpec((1, seq, lanes), lambda b, h, i: (b, 0, h))
    return pl.pallas_call(
        _attn_kernel,
        grid=(bsz, d // lanes, seq // tq),
        in_specs=[qspec, kvspec, kvspec, pl.BlockSpec((2 * tq, 2 * tq), lambda b, h, i: (0, 0))],
        out_specs=qspec,
        out_shape=jax.ShapeDtypeStruct(q.shape, BF16),
        scratch_shapes=[pltpu.VMEM((2, tq, lanes), F32), pltpu.VMEM((2, tq, tq), F32)],
        compiler_params=pltpu.CompilerParams(
            dimension_semantics=("arbitrary", "arbitrary", "arbitrary"), vmem_limit_bytes=VMEM_LIMIT),
    )(q, k, v, tri)


def _proj_kernel(x_ref, a_ref, w_ref, o_ref):
    o_ref[0] = x_ref[0] + _dot(a_ref[0], w_ref[...])


def _residual_proj(x, a, w, *, ts):
    bsz, seq, d = x.shape
    tile = pl.BlockSpec((1, ts, d), lambda b, s: (b, s, 0))
    atile = pl.BlockSpec((1, ts, a.shape[-1]), lambda b, s: (b, s, 0))
    return pl.pallas_call(
        _proj_kernel,
        grid=(bsz, seq // ts),
        in_specs=[tile, atile, _const_spec(w.shape)],
        out_specs=tile,
        out_shape=jax.ShapeDtypeStruct(x.shape, x.dtype),
        compiler_params=_cparams(),
    )(x, a, w.astype(BF16))


def _sb_layer(x, g, w_qkv, q_g, k_g, w_o, *, ts):
    q, k, v = _qkv_proj(x, g, w_qkv, q_g, k_g, ts=ts)
    return _residual_proj(x, _attention(q, k, v), w_o, ts=ts)


def _tile_rows(seq, want):
    ts = min(seq, want)
    assert seq % ts == 0 and ts % HALO == 0, (seq, ts)
    return ts


def kernel(x, norm_mix_g, norm_ffn_g, pool_w, pool_b, pool_scale, s5_lam_re, s5_lam_im, s5_log_dt, s5_b_re, s5_b_im, s5_c_re, s5_c_im, s5_d, s5_w_out, s5_b_out, lru_w_in, lru_conv_w, lru_conv_b, lru_w_a, lru_b_a, lru_w_x, lru_b_x, lru_lam, lru_w_out, sb_w_qkv, sb_q_g, sb_k_g, sb_w_o, ffn_w_in, ffn_conv_w, ffn_conv_b, ffn_w_out):
    depth = norm_mix_g.shape[0]
    seq = x.shape[1]
    for layer in range(depth):
        m, j = layer % 4, layer // 4
        g = norm_mix_g[layer]
        if m == 0:
            x = _pool_layer(x, g, pool_w[j], pool_b[j], pool_scale[j], ts=_tile_rows(seq, 512))
        elif m == 1:
            x = _s5_layer(x, g, s5_lam_re[j], s5_lam_im[j], s5_log_dt[j], s5_b_re[j], s5_b_im[j],
                          s5_c_re[j], s5_c_im[j], s5_d[j], s5_w_out[j], s5_b_out[j],
                          ts=_tile_rows(seq, 256))
        elif m == 2:
            x = _lru_layer(x, g, lru_w_in[j], lru_conv_w[j], lru_conv_b[j], lru_w_a[j], lru_b_a[j],
                           lru_w_x[j], lru_b_x[j], lru_lam[j], lru_w_out[j], ts=_tile_rows(seq, 512))
        else:
            x = _sb_layer(x, g, sb_w_qkv[j], sb_q_g[j], sb_k_g[j], sb_w_o[j], ts=_tile_rows(seq, 512))
        x = _conv_ffn(x, norm_ffn_g[layer], ffn_w_in[layer], ffn_conv_w[layer], ffn_conv_b[layer],
                      ffn_w_out[layer], ts=_tile_rows(seq, 512))
    return x
```

```python
import functools
import math

import jax
import jax.numpy as jnp
from jax import lax
from jax.experimental import pallas as pl
from jax.experimental.pallas import tpu as pltpu

F32 = jnp.float32
BF16 = jnp.bfloat16

EPS = 1e-6
HALO = 16
POOL_WINDOWS = (2, 4, 8, 16)
S5_GROUP = 16
S5_STATE = 64
S5_SLAB = 256
LRU_BLOCKS = 4
LRU_CONV = 4
LRU_C = 8.0
SB_HEAD_DIM = 64
FFN_CONV = 3
FFN_CHUNK = 256
VMEM_LIMIT = 56 * 1024 * 1024


def _cparams():
    return pltpu.CompilerParams(
        dimension_semantics=("arbitrary", "arbitrary"),
        vmem_limit_bytes=VMEM_LIMIT)


def _rms(x, g):
    ms = jnp.mean(x * x, axis=-1, keepdims=True)
    return x * lax.rsqrt(ms + EPS) * g


def _gelu(x):
    c = math.sqrt(2.0 / math.pi)
    return 0.5 * x * (1.0 + jnp.tanh(c * (x + 0.044715 * (x * x * x))))


def _sigmoid(x):
    return 1.0 / (1.0 + jnp.exp(-x))


def _softplus(x):
    return jnp.maximum(x, 0.0) + jnp.log(1.0 + jnp.exp(-jnp.abs(x)))


def _dot(a, b):
    return jnp.dot(a, b, preferred_element_type=F32)


def _split(x):
    hi = x.astype(BF16)
    lo = (x - hi.astype(F32)).astype(BF16)
    return hi, lo


def _normed_with_halo(xh_ref, x_ref, g, xn_ref):
    first = pl.program_id(1) == 0
    hn = _rms(xh_ref[0], g)
    xn_ref[0:HALO, :] = jnp.where(first, 0.0, hn).astype(xn_ref.dtype)
    xn_ref[HALO:, :] = _rms(x_ref[0], g).astype(xn_ref.dtype)


def _tile_specs(ts, d):
    per = ts // HALO
    halo = pl.BlockSpec((1, HALO, d), lambda b, s: (b, jnp.maximum(s * per - 1, 0), 0))
    tile = pl.BlockSpec((1, ts, d), lambda b, s: (b, s, 0))
    return halo, tile


def _const_spec(shape):
    zeros = (0,) * len(shape)
    return pl.BlockSpec(shape, lambda b, s: zeros, pipeline_mode=pl.Buffered(1))


def _ffn_kernel(xh_ref, x_ref, g_ref, win_ref, cw_ref, cb_ref, wout_ref, o_ref,
                xn_ref, h_ref, a_ref, *, ts, hidden):
    _normed_with_halo(xh_ref, x_ref, g_ref[...], xn_ref)
    for c in range(hidden // FFN_CHUNK):
        conv = []
        for part in range(2):
            col = part * hidden + c * FFN_CHUNK
            h_ref[part] = _dot(xn_ref[...], win_ref[:, col:col + FFN_CHUNK])
            y = cb_ref[:, col:col + FFN_CHUNK]
            for k in range(FFN_CONV):
                off = HALO - (FFN_CONV - 1) + k
                y = y + cw_ref[k:k + 1, col:col + FFN_CHUNK] * h_ref[part, off:off + ts, :]
            conv.append(y)
        val, gate = conv
        a_ref[:, c * FFN_CHUNK:(c + 1) * FFN_CHUNK] = (gate * _sigmoid(gate) * val).astype(BF16)
    o_ref[0] = x_ref[0] + _dot(a_ref[...], wout_ref[...])


def _conv_ffn(x, g, w_in, conv_w, conv_b, w_out, *, ts):
    bsz, seq, d = x.shape
    hidden = w_out.shape[0]
    halo, tile = _tile_specs(ts, d)
    return pl.pallas_call(
        functools.partial(_ffn_kernel, ts=ts, hidden=hidden),
        grid=(bsz, seq // ts),
        in_specs=[halo, tile, _const_spec((1, d)), _const_spec((d, 2 * hidden)),
                  _const_spec((FFN_CONV, 2 * hidden)), _const_spec((1, 2 * hidden)),
                  _const_spec((hidden, d))],
        out_specs=tile,
        out_shape=jax.ShapeDtypeStruct(x.shape, x.dtype),
        scratch_shapes=[pltpu.VMEM((HALO + ts, d), BF16),
                        pltpu.VMEM((2, HALO + ts, FFN_CHUNK), F32),
                        pltpu.VMEM((ts, hidden), BF16)],
        compiler_params=_cparams(),
    )(x, x, g.reshape(1, d), w_in.astype(BF16), conv_w, conv_b.reshape(1, -1), w_out.astype(BF16))


def _pool_kernel(xh_ref, x_ref, g_ref, w_ref, b_ref, sc_ref, o_ref, hn_ref, *, ts, group):
    _normed_with_halo(xh_ref, x_ref, g_ref[...], hn_ref)
    pos = pl.program_id(1) * ts + lax.broadcasted_iota(jnp.int32, (ts, 1), 0)
    for gi, win in enumerate(POOL_WINDOWS):
        lanes = slice(gi * group, (gi + 1) * group)
        h = hn_ref[HALO:HALO + ts, lanes]
        acc = h
        for k in range(1, win):
            acc = acc + hn_ref[HALO - k:HALO - k + ts, lanes]
        count = jnp.minimum(pos + 1, win).astype(F32)
        diff = acc / count - h
        y = _dot(diff.astype(BF16), w_ref[gi]) + b_ref[:, lanes]
        o_ref[0, :, lanes] = x_ref[0, :, lanes] + sc_ref[:, lanes] * y


def _pool_layer(x, g, w, b, scale, *, ts):
    bsz, seq, d = x.shape
    ngroup, group = w.shape[0], w.shape[1]
    halo, tile = _tile_specs(ts, d)
    return pl.pallas_call(
        functools.partial(_pool_kernel, ts=ts, group=group),
        grid=(bsz, seq // ts),
        in_specs=[halo, tile, _const_spec((1, d)), _const_spec((ngroup, group, group)),
                  _const_spec((1, d)), _const_spec((1, d))],
        out_specs=tile,
        out_shape=jax.ShapeDtypeStruct(x.shape, x.dtype),
        scratch_shapes=[pltpu.VMEM((HALO + ts, d), F32)],
        compiler_params=_cparams(),
    )(x, x, g.reshape(1, d), w.astype(BF16), b.reshape(1, d), scale.reshape(1, d))


S5_LANE_CHUNK = 256


def _s5_kernel(x_ref, g_ref, wbr_ref, wbi_ref, pw_ref, wcr_ref, wci_ref, dsk_ref, wout_ref, bout_ref,
               o_ref, u_ref, sre_ref, sim_ref, cre_ref, cim_ref, y_ref, *, ts, d):
    nslab = d // S5_SLAB
    width = sre_ref.shape[1] // nslab
    u = _rms(x_ref[0], g_ref[...])
    u_ref[...] = u
    ub = u.astype(BF16)
    for j in range(nslab):
        us = ub[:, j * S5_SLAB:(j + 1) * S5_SLAB]
        sre_ref[:, j * width:(j + 1) * width] = _dot(us, wbr_ref[j])
        sim_ref[:, j * width:(j + 1) * width] = _dot(us, wbi_ref[j])

    @pl.when(pl.program_id(1) == 0)
    def _():
        cre_ref[...] = jnp.zeros_like(cre_ref)
        cim_ref[...] = jnp.zeros_like(cim_ref)

    lc = S5_LANE_CHUNK
    for c in range(sre_ref.shape[1] // lc):
        lanes = slice(c * lc, (c + 1) * lc)
        pws = [(pw_ref[i, 0, :, lanes], pw_ref[i, 1, :, lanes]) for i in range(4)]

        def body(i, carry, lanes=lanes, pws=pws):
            c_re, c_im = carry
            r0 = pl.multiple_of(i * 8, 8)
            b_re = sre_ref[pl.ds(r0, 8), lanes]
            b_im = sim_ref[pl.ds(r0, 8), lanes]
            for step, shift in enumerate((1, 2, 4)):
                p_re, p_im = pws[step]
                s_re = pltpu.roll(b_re, shift, 0)
                s_im = pltpu.roll(b_im, shift, 0)
                b_re, b_im = (b_re + (p_re * s_re - p_im * s_im),
                              b_im + (p_re * s_im + p_im * s_re))
            p_re, p_im = pws[3]
            h_re = b_re + (p_re * c_re - p_im * c_im)
            h_im = b_im + (p_re * c_im + p_im * c_re)
            sre_ref[pl.ds(r0, 8), lanes] = h_re
            sim_ref[pl.ds(r0, 8), lanes] = h_im
            return (jnp.broadcast_to(h_re[7:8, :], (8, lc)), jnp.broadcast_to(h_im[7:8, :], (8, lc)))

        c_re, c_im = lax.fori_loop(0, ts // 8, body, (cre_ref[:, lanes], cim_ref[:, lanes]), unroll=2)
        cre_ref[:, lanes] = c_re
        cim_ref[:, lanes] = c_im

    for j in range(nslab):
        cols = slice(j * width, (j + 1) * width)
        lanes = slice(j * S5_SLAB, (j + 1) * S5_SLAB)
        y = (_dot(sre_ref[:, cols].astype(BF16), wcr_ref[j])
             + _dot(sim_ref[:, cols].astype(BF16), wci_ref[j]))
        y = y + dsk_ref[:, lanes] * u_ref[:, lanes]
        y_ref[:, lanes] = _gelu(y).astype(BF16)
    z = _dot(y_ref[...], wout_ref[...]) + bout_ref[...]
    o_ref[0] = x_ref[0] + z[:, :d] * _sigmoid(z[:, d:])


def _s5_tables(lam_re, lam_im, log_dt, b_re, b_im, c_re, c_im, d):
    ngroups, nstate = lam_re.shape
    per_slab = S5_SLAB // S5_GROUP
    nslab = d // S5_SLAB
    lam = lax.complex(jnp.minimum(lam_re.astype(F32), -1e-4), lam_im.astype(F32))
    dt = jnp.exp(log_dt.astype(F32))[:, None]
    lam_bar = jnp.exp(lam * dt)
    b_bar = ((lam_bar - 1.0) / lam)[..., None] * lax.complex(b_re.astype(F32), b_im.astype(F32))
    eye = jnp.eye(per_slab, dtype=F32)

    def b_slabs(t):
        t = t.reshape(nslab, per_slab, nstate, S5_GROUP)
        return jnp.einsum('jgph,gk->jghkp', t, eye).reshape(nslab, S5_SLAB, per_slab * nstate)

    def c_slabs(t):
        t = t.reshape(nslab, per_slab, S5_GROUP, nstate)
        return jnp.einsum('jghp,gk->jgpkh', t, eye).reshape(nslab, per_slab * nstate, S5_SLAB)

    wbr = b_slabs(jnp.real(b_bar)).astype(BF16)
    wbi = b_slabs(jnp.imag(b_bar)).astype(BF16)
    wcr = c_slabs(c_re.astype(F32)).astype(BF16)
    wci = c_slabs(-c_im.astype(F32)).astype(BF16)

    lam1 = lam_bar.reshape(1, -1)
    lam2 = lam1 * lam1
    lam4 = lam2 * lam2
    row = jnp.arange(8)[:, None]
    steps = [jnp.where(row >= s, p, 0.0) for s, p in ((1, lam1), (2, lam2), (4, lam4))]
    ramp = [lam1]
    for _ in range(7):
        ramp.append(ramp[-1] * lam1)
    steps.append(jnp.concatenate(ramp, axis=0))
    pw = jnp.stack(steps)
    pw = jnp.stack([jnp.real(pw), jnp.imag(pw)], axis=1).astype(F32)
    return wbr, wbi, pw, wcr, wci


def _s5_layer(x, g, lam_re, lam_im, log_dt, b_re, b_im, c_re, c_im, d_skip, w_out, b_out, *, ts):
    bsz, seq, d = x.shape
    wbr, wbi, pw, wcr, wci = _s5_tables(lam_re, lam_im, log_dt, b_re, b_im, c_re, c_im, d)
    nstate_cols = pw.shape[-1]
    tile = pl.BlockSpec((1, ts, d), lambda b, s: (b, s, 0))
    return pl.pallas_call(
        functools.partial(_s5_kernel, ts=ts, d=d),
        grid=(bsz, seq // ts),
        in_specs=[tile, _const_spec((1, d)), _const_spec(wbr.shape), _const_spec(wbi.shape),
                  _const_spec(pw.shape), _const_spec(wcr.shape), _const_spec(wci.shape),
                  _const_spec((1, d)), _const_spec((d, 2 * d)), _const_spec((1, 2 * d))],
        out_specs=tile,
        out_shape=jax.ShapeDtypeStruct(x.shape, x.dtype),
        scratch_shapes=[pltpu.VMEM((ts, d), F32),
                        pltpu.VMEM((ts, nstate_cols), F32), pltpu.VMEM((ts, nstate_cols), F32),
                        pltpu.VMEM((8, nstate_cols), F32), pltpu.VMEM((8, nstate_cols), F32),
                        pltpu.VMEM((ts, d), BF16)],
        compiler_params=_cparams(),
    )(x, g.reshape(1, d), wbr, wbi, pw, wcr, wci, d_skip.reshape(1, d), w_out.astype(BF16),
      b_out.reshape(1, 2 * d))


def _lru_kernel(xh_ref, x_ref, g_ref, win_ref, cw_ref, cb_ref, wa_ref, ba_ref, wx_ref, bx_ref, lam_ref,
                wout_ref, o_ref, xn_ref, gb_ref, rec_ref, a_ref, b_ref, carry_ref, *, ts, width):
    _normed_with_halo(xh_ref, x_ref, g_ref[...], xn_ref)
    gb_ref[...] = _dot(xn_ref[HALO:, :], win_ref[:, :width])
    rec_ref[...] = _dot(xn_ref[...], win_ref[:, width:])
    block = width // LRU_BLOCKS
    for n in range(LRU_BLOCKS):
        lanes = slice(n * block, (n + 1) * block)
        rec = cb_ref[:, lanes]
        for k in range(LRU_CONV):
            off = HALO - (LRU_CONV - 1) + k
            rec = rec + cw_ref[k:k + 1, lanes] * rec_ref[off:off + ts, lanes]
        rb = rec.astype(BF16)
        r = _sigmoid(_dot(rb, wa_ref[n]) + ba_ref[:, lanes])
        i = _sigmoid(_dot(rb, wx_ref[n]) + bx_ref[:, lanes])
        log_a = (-LRU_C * r) * _softplus(-lam_ref[:, lanes])
        a_ref[:, lanes] = jnp.exp(log_a)
        th = jnp.tanh(log_a)
        b_ref[:, lanes] = jnp.sqrt(-2.0 * th / (1.0 - th)) * (i * rec)

    @pl.when(pl.program_id(1) == 0)
    def _():
        carry_ref[...] = jnp.zeros_like(carry_ref)

    row = lax.broadcasted_iota(jnp.int32, (8, width), 0)

    def body(i, carry):
        r0 = pl.multiple_of(i * 8, 8)
        a = a_ref[pl.ds(r0, 8), :]
        b = b_ref[pl.ds(r0, 8), :]
        for shift in (1, 2, 4):
            has_partner = row >= shift
            a_prev = jnp.where(has_partner, pltpu.roll(a, shift, 0), 1.0)
            b_prev = jnp.where(has_partner, pltpu.roll(b, shift, 0), 0.0)
            b = a * b_prev + b
            a = a * a_prev
        h = a * carry + b
        b_ref[pl.ds(r0, 8), :] = h
        return jnp.broadcast_to(h[7:8, :], (8, width))

    carry_ref[...] = lax.fori_loop(0, ts // 8, body, carry_ref[...], unroll=2)
    y = (_gelu(gb_ref[...]) * b_ref[...]).astype(BF16)
    o_ref[0] = x_ref[0] + _dot(y, wout_ref[...])


def _lru_layer(x, g, w_in, conv_w, conv_b, w_a, b_a, w_x, b_x, lam, w_out, *, ts):
    bsz, seq, d = x.shape
    width = w_out.shape[0]
    halo, tile = _tile_specs(ts, d)
    row = lambda v: v.reshape(1, -1)
    return pl.pallas_call(
        functools.partial(_lru_kernel, ts=ts, width=width),
        grid=(bsz, seq // ts),
        in_specs=[halo, tile, _const_spec((1, d)), _const_spec((d, 2 * width)),
                  _const_spec((LRU_CONV, width)), _const_spec((1, width)),
                  _const_spec(w_a.shape), _const_spec((1, width)),
                  _const_spec(w_x.shape), _const_spec((1, width)), _const_spec((1, width)),
                  _const_spec((width, d))],
        out_specs=tile,
        out_shape=jax.ShapeDtypeStruct(x.shape, x.dtype),
        scratch_shapes=[pltpu.VMEM((HALO + ts, d), BF16),
                        pltpu.VMEM((ts, width), F32),
                        pltpu.VMEM((HALO + ts, width), F32),
                        pltpu.VMEM((ts, width), F32), pltpu.VMEM((ts, width), F32),
                        pltpu.VMEM((8, width), F32)],
        compiler_params=_cparams(),
    )(x, x, row(g), w_in.astype(BF16), conv_w, row(conv_b), w_a.astype(BF16), row(b_a),
      w_x.astype(BF16), row(b_x), row(lam), w_out.astype(BF16))


def _qkv_kernel(x_ref, g_ref, w_ref, qg_ref, kg_ref, gsum_ref, gexp_ref, q_ref, k_ref, v_ref, *, d):
    xn = _rms(x_ref[0], g_ref[...]).astype(BF16)

    def head_normed(t, gain):
        hi, lo = _split(t * t)
        ms = _dot(hi, gsum_ref[...]) + _dot(lo, gsum_ref[...])
        hi, lo = _split(lax.rsqrt(ms + EPS))
        return t * (_dot(hi, gexp_ref[...]) + _dot(lo, gexp_ref[...])) * gain

    scale = 1.0 / math.sqrt(SB_HEAD_DIM)
    q_ref[0] = (head_normed(_dot(xn, w_ref[:, :d]), qg_ref[...]) * scale).astype(BF16)
    k_ref[0] = head_normed(_dot(xn, w_ref[:, d:2 * d]), kg_ref[...]).astype(BF16)
    v_ref[0] = _dot(xn, w_ref[:, 2 * d:]).astype(BF16)


def _qkv_proj(x, g, w_qkv, q_g, k_g, *, ts):
    bsz, seq, d = x.shape
    heads = d // SB_HEAD_DIM
    head_of_lane = jnp.arange(d) // SB_HEAD_DIM
    onehot = (head_of_lane[:, None] == jnp.arange(128)[None, :])
    gsum = (onehot.astype(F32) / SB_HEAD_DIM).astype(BF16)
    gexp = onehot.T.astype(BF16)
    tile = pl.BlockSpec((1, ts, d), lambda b, s: (b, s, 0))
    out = jax.ShapeDtypeStruct(x.shape, BF16)
    return pl.pallas_call(
        functools.partial(_qkv_kernel, d=d),
        grid=(bsz, seq // ts),
        in_specs=[tile, _const_spec((1, d)), _const_spec((d, 3 * d)), _const_spec((1, d)),
                  _const_spec((1, d)), _const_spec((d, 128)), _const_spec((128, d))],
        out_specs=[tile, tile, tile],
        out_shape=[out, out, out],
        compiler_params=_cparams(),
    )(x, g.reshape(1, d), w_qkv.astype(BF16), jnp.tile(q_g, heads).reshape(1, d),
      jnp.tile(k_g, heads).reshape(1, d), gsum, gexp)


SB_TILE = 256
SB_SUB = 128
SB_PAIRS = 2


def _attn_kernel(q_ref, k_ref, v_ref, tri_ref, o_ref, acc_ref, run_ref):
    tq, sub, pair = SB_TILE, SB_SUB, 2 * SB_HEAD_DIM
    qi = pl.program_id(2)
    lane = lax.broadcasted_iota(jnp.int32, (tq, pair), 1)
    first_head = lane < SB_HEAD_DIM
    t_pos = lax.broadcasted_iota(jnp.int32, (tq, tq), 0)
    s_pos = lax.broadcasted_iota(jnp.int32, (tq, tq), 1)
    causal = s_pos < t_pos
    q_heads = []
    for p in range(SB_PAIRS):
        q = q_ref[0, :, p * pair:(p + 1) * pair]
        zero = jnp.zeros_like(q)
        q_heads += [jnp.where(first_head, q, zero), jnp.where(first_head, zero, q)]

    def block(kb, diagonal):
        k0 = pl.multiple_of(kb * tq, tq)
        for p in range(SB_PAIRS):
            kblk = k_ref[0, pl.ds(k0, tq), p * pair:(p + 1) * pair]
            vblk = v_ref[0, pl.ds(k0, tq), p * pair:(p + 1) * pair]
            for h in (2 * p, 2 * p + 1):
                z = lax.dot_general(q_heads[h], kblk, (((1,), (1,)), ((), ())), preferred_element_type=F32)
                lsm = -_softplus(z)
                if diagonal:
                    lsm = jnp.where(causal, lsm, 0.0)
                hi, lo = _split(lsm)
                run = run_ref[h]
                parts = []
                for c in reversed(range(tq // sub)):
                    cols = slice(c * sub, (c + 1) * sub)
                    sums = _dot(jnp.concatenate([hi[:, cols], lo[:, cols]], axis=1), tri_ref[...])
                    parts.append(jnp.exp(z[:, cols] + lsm[:, cols] + sums[:, :sub] + run))
                    run = run + sums[:, sub:]
                run_ref[h] = run
                att = jnp.concatenate(parts[::-1], axis=1)
                if diagonal:
                    att = jnp.where(causal, att, 0.0)
                acc_ref[h] += _dot(att.astype(BF16), vblk)

    acc_ref[...] = jnp.zeros_like(acc_ref)
    run_ref[...] = jnp.zeros_like(run_ref)
    block(qi, True)

    def earlier(j, carry):
        block(qi - 1 - j, False)
        return carry

    lax.fori_loop(0, qi, earlier, 0)
    for p in range(SB_PAIRS):
        o_ref[0, :, p * pair:(p + 1) * pair] = jnp.where(
            first_head, acc_ref[2 * p], acc_ref[2 * p + 1]).astype(BF16)


def _attention(q, k, v):
    bsz, seq, d = q.shape
    tq, sub = SB_TILE, SB_SUB
    lanes = SB_PAIRS * 2 * SB_HEAD_DIM
    j = jnp.arange(2 * sub)[:, None] % sub
    s = jnp.arange(2 * sub)[None, :]
    tri = ((s >= sub) | (j > s)).astype(BF16)
    qspec = pl.BlockSpec((1, tq, lanes), lambda b, h, i: (b, i, h))
    kvspec = pl.BlockSpec((1, seq, lanes), lambda b, h, i: (b, 0, h))
    return pl.pallas_call(
        _attn_kernel,
        grid=(bsz, d // lanes, seq // tq),
        in_specs=[qspec, kvspec, kvspec, pl.BlockSpec((2 * sub, 2 * sub), lambda b, h, i: (0, 0))],
        out_specs=qspec,
        out_shape=jax.ShapeDtypeStruct(q.shape, BF16),
        scratch_shapes=[pltpu.VMEM((2 * SB_PAIRS, tq, 2 * SB_HEAD_DIM), F32),
                        pltpu.VMEM((2 * SB_PAIRS, tq, sub), F32)],
        compiler_params=pltpu.CompilerParams(
            dimension_semantics=("arbitrary", "arbitrary", "arbitrary"), vmem_limit_bytes=VMEM_LIMIT),
    )(q, k, v, tri)


def _proj_kernel(x_ref, a_ref, w_ref, o_ref):
    o_ref[0] = x_ref[0] + _dot(a_ref[0], w_ref[...])


def _residual_proj(x, a, w, *, ts):
    bsz, seq, d = x.shape
    tile = pl.BlockSpec((1, ts, d), lambda b, s: (b, s, 0))
    atile = pl.BlockSpec((1, ts, a.shape[-1]), lambda b, s: (b, s, 0))
    return pl.pallas_call(
        _proj_kernel,
        grid=(bsz, seq // ts),
        in_specs=[tile, atile, _const_spec(w.shape)],
        out_specs=tile,
        out_shape=jax.ShapeDtypeStruct(x.shape, x.dtype),
        compiler_params=_cparams(),
    )(x, a, w.astype(BF16))


def _sb_layer(x, g, w_qkv, q_g, k_g, w_o, *, ts):
    q, k, v = _qkv_proj(x, g, w_qkv, q_g, k_g, ts=ts)
    return _residual_proj(x, _attention(q, k, v), w_o, ts=ts)


def _tile_rows(seq, want):
    ts = min(seq, want)
    assert seq % ts == 0 and ts % HALO == 0, (seq, ts)
    return ts


def kernel(x, norm_mix_g, norm_ffn_g, pool_w, pool_b, pool_scale, s5_lam_re, s5_lam_im, s5_log_dt, s5_b_re, s5_b_im, s5_c_re, s5_c_im, s5_d, s5_w_out, s5_b_out, lru_w_in, lru_conv_w, lru_conv_b, lru_w_a, lru_b_a, lru_w_x, lru_b_x, lru_lam, lru_w_out, sb_w_qkv, sb_q_g, sb_k_g, sb_w_o, ffn_w_in, ffn_conv_w, ffn_conv_b, ffn_w_out):
    depth = norm_mix_g.shape[0]
    seq = x.shape[1]
    for layer in range(depth):
        m, j = layer % 4, layer // 4
        g = norm_mix_g[layer]
        if m == 0:
            x = _pool_layer(x, g, pool_w[j], pool_b[j], pool_scale[j], ts=_tile_rows(seq, 512))
        elif m == 1:
            x = _s5_layer(x, g, s5_lam_re[j], s5_lam_im[j], s5_log_dt[j], s5_b_re[j], s5_b_im[j],
                          s5_c_re[j], s5_c_im[j], s5_d[j], s5_w_out[j], s5_b_out[j],
                          ts=_tile_rows(seq, 256))
        elif m == 2:
            x = _lru_layer(x, g, lru_w_in[j], lru_conv_w[j], lru_conv_b[j], lru_w_a[j], lru_b_a[j],
                           lru_w_x[j], lru_b_x[j], lru_lam[j], lru_w_out[j], ts=_tile_rows(seq, 512))
        else:
            x = _sb_layer(x, g, sb_w_qkv[j], sb_q_g[j], sb_k_g[j], sb_w_o[j], ts=_tile_rows(seq, 512))
        x = _conv_ffn(x, norm_ffn_g[layer], ffn_w_in[layer], ffn_conv_w[layer], ffn_conv_b[layer],
                      ffn_w_out[layer], ts=_tile_rows(seq, 512))
    return x
```

```python
import functools
import math

import jax
import jax.numpy as jnp
from jax import lax
from jax.experimental import pallas as pl
from jax.experimental.pallas import tpu as pltpu

F32 = jnp.float32
BF16 = jnp.bfloat16

EPS = 1e-6
LOG2E = 1.0 / math.log(2.0)
HALO = 16
POOL_WINDOWS = (2, 4, 8, 16)
S5_GROUP = 16
S5_STATE = 64
S5_SLAB = 128
LRU_BLOCKS = 4
LRU_CONV = 4
LRU_C = 8.0
SB_HEAD_DIM = 64
FFN_CONV = 3
FFN_CHUNK = 256
VMEM_LIMIT = 56 * 1024 * 1024


def _cparams():
    return pltpu.CompilerParams(
        dimension_semantics=("arbitrary", "arbitrary"),
        vmem_limit_bytes=VMEM_LIMIT)


def _rms(x, g):
    ms = jnp.mean(x * x, axis=-1, keepdims=True)
    return x * lax.rsqrt(ms + EPS) * g


def _gelu(x):
    c = math.sqrt(2.0 / math.pi)
    return 0.5 * x * (1.0 + jnp.tanh(c * (x + 0.044715 * (x * x * x))))


def _sigmoid(x):
    return 1.0 / (1.0 + jnp.exp(-x))


def _softplus(x):
    return jnp.maximum(x, 0.0) + jnp.log(1.0 + jnp.exp(-jnp.abs(x)))


def _dot(a, b):
    return jnp.dot(a, b, preferred_element_type=F32)


def _split(x):
    hi = x.astype(BF16)
    lo = (x - hi.astype(F32)).astype(BF16)
    return hi, lo


def _normed_with_halo(xh_ref, x_ref, g, xn_ref):
    first = pl.program_id(1) == 0
    hn = _rms(xh_ref[0], g)
    xn_ref[0:HALO, :] = jnp.where(first, 0.0, hn).astype(xn_ref.dtype)
    xn_ref[HALO:, :] = _rms(x_ref[0], g).astype(xn_ref.dtype)


def _tile_specs(ts, d):
    per = ts // HALO
    halo = pl.BlockSpec((1, HALO, d), lambda b, s: (b, jnp.maximum(s * per - 1, 0), 0))
    tile = pl.BlockSpec((1, ts, d), lambda b, s: (b, s, 0))
    return halo, tile


def _const_spec(shape):
    zeros = (0,) * len(shape)
    return pl.BlockSpec(shape, lambda b, s: zeros, pipeline_mode=pl.Buffered(1))


def _ffn_kernel(xh_ref, x_ref, g_ref, win_ref, cw_ref, cb_ref, wout_ref, o_ref,
                xn_ref, h_ref, a_ref, *, ts, hidden):
    _normed_with_halo(xh_ref, x_ref, g_ref[...], xn_ref)
    for c in range(hidden // FFN_CHUNK):
        conv = []
        for part in range(2):
            col = part * hidden + c * FFN_CHUNK
            slot = 2 * (c % 2) + part
            h_ref[slot] = _dot(xn_ref[...], win_ref[:, col:col + FFN_CHUNK])
            y = cb_ref[:, col:col + FFN_CHUNK]
            for k in range(FFN_CONV):
                off = HALO - (FFN_CONV - 1) + k
                y = y + cw_ref[k:k + 1, col:col + FFN_CHUNK] * h_ref[slot, off:off + ts, :]
            conv.append(y)
        val, gate = conv
        a_ref[:, c * FFN_CHUNK:(c + 1) * FFN_CHUNK] = (gate * _sigmoid(gate) * val).astype(BF16)
    o_ref[0] = x_ref[0] + _dot(a_ref[...], wout_ref[...])


def _conv_ffn(x, g, w_in, conv_w, conv_b, w_out, *, ts):
    bsz, seq, d = x.shape
    hidden = w_out.shape[0]
    halo, tile = _tile_specs(ts, d)
    return pl.pallas_call(
        functools.partial(_ffn_kernel, ts=ts, hidden=hidden),
        grid=(bsz, seq // ts),
        in_specs=[halo, tile, _const_spec((1, d)), _const_spec((d, 2 * hidden)),
                  _const_spec((FFN_CONV, 2 * hidden)), _const_spec((1, 2 * hidden)),
                  _const_spec((hidden, d))],
        out_specs=tile,
        out_shape=jax.ShapeDtypeStruct(x.shape, x.dtype),
        scratch_shapes=[pltpu.VMEM((HALO + ts, d), BF16),
                        pltpu.VMEM((4, HALO + ts, FFN_CHUNK), F32),
                        pltpu.VMEM((ts, hidden), BF16)],
        compiler_params=_cparams(),
    )(x, x, g.reshape(1, d), w_in.astype(BF16), conv_w, conv_b.reshape(1, -1), w_out.astype(BF16))


def _pool_kernel(xh_ref, x_ref, g_ref, w_ref, b_ref, sc_ref, o_ref, hn_ref, *, ts, group):
    _normed_with_halo(xh_ref, x_ref, g_ref[...], hn_ref)
    pos = pl.program_id(1) * ts + lax.broadcasted_iota(jnp.int32, (ts, 1), 0)
    for gi, win in enumerate(POOL_WINDOWS):
        lanes = slice(gi * group, (gi + 1) * group)
        h = hn_ref[HALO:HALO + ts, lanes]
        acc = h
        for k in range(1, win):
            acc = acc + hn_ref[HALO - k:HALO - k + ts, lanes]
        count = jnp.minimum(pos + 1, win).astype(F32)
        diff = acc / count - h
        y = _dot(diff.astype(BF16), w_ref[gi]) + b_ref[:, lanes]
        o_ref[0, :, lanes] = x_ref[0, :, lanes] + sc_ref[:, lanes] * y


def _pool_layer(x, g, w, b, scale, *, ts):
    bsz, seq, d = x.shape
    ngroup, group = w.shape[0], w.shape[1]
    halo, tile = _tile_specs(ts, d)
    return pl.pallas_call(
        functools.partial(_pool_kernel, ts=ts, group=group),
        grid=(bsz, seq // ts),
        in_specs=[halo, tile, _const_spec((1, d)), _const_spec((ngroup, group, group)),
                  _const_spec((1, d)), _const_spec((1, d))],
        out_specs=tile,
        out_shape=jax.ShapeDtypeStruct(x.shape, x.dtype),
        scratch_shapes=[pltpu.VMEM((HALO + ts, d), F32)],
        compiler_params=_cparams(),
    )(x, x, g.reshape(1, d), w.astype(BF16), b.reshape(1, d), scale.reshape(1, d))


S5_CHUNK = 8
S5_LANE_CHUNK = 256


def _s5_norm_kernel(x_ref, g_ref, u_ref):
    u_ref[0] = _rms(x_ref[0], g_ref[...])


def _s5_core_kernel(u_ref, wz_ref, tz_ref, wexp_ref, pw_ref, dsk_ref, o_ref, xc_ref, z_ref, y_ref, *, seq):
    m = seq // S5_CHUNK
    sl = S5_SLAB
    for r in range(S5_CHUNK):
        xc_ref[:, r * sl:(r + 1) * sl] = u_ref[0, pl.ds(r, m, stride=S5_CHUNK), :].astype(BF16)
    half = z_ref.shape[1] // 2
    z_ref[0:8, :] = jnp.zeros((8, 2 * half), F32)
    z_ref[8:, :] = _dot(xc_ref[...], wz_ref[0])

    lc = S5_LANE_CHUNK
    for c in range(half // lc):
        re = slice(c * lc, (c + 1) * lc)
        im = slice(half + c * lc, half + (c + 1) * lc)
        pws = [(pw_ref[0, i, 0, :, re], pw_ref[0, i, 1, :, re]) for i in range(4)]

        def body(i, carry, re=re, im=im, pws=pws):
            c_re, c_im = carry
            r0 = pl.multiple_of(8 + i * 8, 8)
            b_re = z_ref[pl.ds(r0, 8), re]
            b_im = z_ref[pl.ds(r0, 8), im]
            for step, shift in enumerate((1, 2, 4)):
                p_re, p_im = pws[step]
                s_re = pltpu.roll(b_re, shift, 0)
                s_im = pltpu.roll(b_im, shift, 0)
                b_re, b_im = (b_re + (p_re * s_re - p_im * s_im),
                              b_im + (p_re * s_im + p_im * s_re))
            p_re, p_im = pws[3]
            h_re = b_re + (p_re * c_re - p_im * c_im)
            h_im = b_im + (p_re * c_im + p_im * c_re)
            z_ref[pl.ds(r0, 8), re] = h_re
            z_ref[pl.ds(r0, 8), im] = h_im
            return (jnp.broadcast_to(h_re[7:8, :], (8, lc)), jnp.broadcast_to(h_im[7:8, :], (8, lc)))

        zero = jnp.zeros((8, lc), F32)
        lax.fori_loop(0, m // 8, body, (zero, zero), unroll=2)

    before = z_ref[7:7 + m, :].astype(BF16)
    yc = _dot(xc_ref[...], tz_ref[0]) + _dot(before, wexp_ref[0])
    for r in range(S5_CHUNK):
        rows = pl.ds(r, m, stride=S5_CHUNK)
        y_ref[rows, :] = _gelu(yc[:, r * sl:(r + 1) * sl] + dsk_ref[...] * u_ref[0, rows, :])
    o_ref[0] = y_ref[...].astype(BF16)


def _s5_out_kernel(x_ref, y_ref, w_ref, b_ref, o_ref, *, d):
    z = _dot(y_ref[0], w_ref[...]) + b_ref[...]
    o_ref[0] = x_ref[0] + z[:, :d] * _sigmoid(z[:, d:])


def _s5_tables(lam_re, lam_im, log_dt, b_re, b_im, c_re, c_im, d):
    ngroups, nstate = lam_re.shape
    per = S5_SLAB // S5_GROUP
    nslab = d // S5_SLAB
    nl = S5_CHUNK
    hh = S5_GROUP
    lam = lax.complex(jnp.minimum(lam_re.astype(F32), -1e-4), lam_im.astype(F32))
    dt = jnp.exp(log_dt.astype(F32))[:, None]
    lam_bar = jnp.exp(lam * dt)
    b_bar = ((lam_bar - 1.0) / lam)[..., None] * lax.complex(b_re.astype(F32), b_im.astype(F32))
    cc = lax.complex(c_re.astype(F32), c_im.astype(F32))
    powers = [jnp.ones_like(lam_bar)]
    for _ in range(nl):
        powers.append(powers[-1] * lam_bar)
    powers = jnp.stack(powers)
    eye = jnp.eye(per, dtype=F32)

    t = powers[nl - 1::-1][:nl, :, :, None] * b_bar[None]
    t = t.reshape(nl, nslab, per, nstate, hh)
    wz = jnp.concatenate(
        [jnp.einsum('rjgph,gk->jrghkp', f(t), eye).reshape(nslab, nl * S5_SLAB, per * nstate)
         for f in (jnp.real, jnp.imag)], axis=2)

    kern = jnp.real(jnp.einsum('gip,kgp,gpj->gkij', cc, powers[:nl], b_bar))
    lag = jnp.arange(nl)[None, :] - jnp.arange(nl)[:, None]
    kl = jnp.where((lag >= 0)[None, :, :, None, None], kern[:, jnp.clip(lag, 0, nl - 1)], 0.0)
    kl = kl.reshape(nslab, per, nl, nl, hh, hh)
    tz = jnp.einsum('jgabih,gk->jaghbki', kl, eye).reshape(nslab, nl * S5_SLAB, nl * S5_SLAB)

    e = cc[None] * powers[1:nl + 1, :, None, :]
    e = e.reshape(nl, nslab, per, hh, nstate)
    wexp = jnp.concatenate(
        [jnp.einsum('rjghp,gk->jgprkh', f(e), eye).reshape(nslab, per * nstate, nl * S5_SLAB)
         for f in (jnp.real, lambda v: -jnp.imag(v))], axis=1)

    step1 = powers[nl].reshape(nslab, 1, per * nstate)
    step2 = step1 * step1
    step4 = step2 * step2
    row = jnp.arange(8)[None, :, None]
    tabs = [jnp.where(row >= s, p, 0.0) for s, p in ((1, step1), (2, step2), (4, step4))]
    ramp = [step1]
    for _ in range(7):
        ramp.append(ramp[-1] * step1)
    tabs.append(jnp.concatenate(ramp, axis=1))
    pw = jnp.stack(tabs, axis=1)
    pw = jnp.stack([jnp.real(pw), jnp.imag(pw)], axis=2).astype(F32)
    return wz.astype(BF16), tz.astype(BF16), wexp.astype(BF16), pw


def _s5_layer(x, g, lam_re, lam_im, log_dt, b_re, b_im, c_re, c_im, d_skip, w_out, b_out, *, ts):
    bsz, seq, d = x.shape
    assert seq % (8 * S5_CHUNK) == 0, seq
    wz, tz, wexp, pw = _s5_tables(lam_re, lam_im, log_dt, b_re, b_im, c_re, c_im, d)
    nslab = d // S5_SLAB
    tile = pl.BlockSpec((1, ts, d), lambda b, s: (b, s, 0))
    u = pl.pallas_call(
        _s5_norm_kernel,
        grid=(bsz, seq // ts),
        in_specs=[tile, _const_spec((1, d))],
        out_specs=tile,
        out_shape=jax.ShapeDtypeStruct(x.shape, F32),
        compiler_params=_cparams(),
    )(x, g.reshape(1, d))

    slab = pl.BlockSpec((1, seq, S5_SLAB), lambda j, b: (b, 0, j))
    m = seq // S5_CHUNK
    cols = pw.shape[-1]

    def per_slab(shape):
        zeros = (0,) * (len(shape) - 1)
        return pl.BlockSpec((1,) + shape[1:], lambda j, b: (j,) + zeros, pipeline_mode=pl.Buffered(1))

    y = pl.pallas_call(
        functools.partial(_s5_core_kernel, seq=seq),
        grid=(nslab, bsz),
        in_specs=[slab, per_slab(wz.shape), per_slab(tz.shape), per_slab(wexp.shape), per_slab(pw.shape),
                  pl.BlockSpec((1, S5_SLAB), lambda j, b: (0, j))],
        out_specs=slab,
        out_shape=jax.ShapeDtypeStruct(x.shape, BF16),
        scratch_shapes=[pltpu.VMEM((m, S5_CHUNK * S5_SLAB), BF16),
                        pltpu.VMEM((8 + m, 2 * cols), F32),
                        pltpu.VMEM((seq, S5_SLAB), F32)],
        compiler_params=_cparams(),
    )(u, wz, tz, wexp, pw, d_skip.reshape(1, d))

    return pl.pallas_call(
        functools.partial(_s5_out_kernel, d=d),
        grid=(bsz, seq // ts),
        in_specs=[tile, tile, _const_spec((d, 2 * d)), _const_spec((1, 2 * d))],
        out_specs=tile,
        out_shape=jax.ShapeDtypeStruct(x.shape, x.dtype),
        compiler_params=_cparams(),
    )(x, y, w_out.astype(BF16), b_out.reshape(1, 2 * d))


def _lru_kernel(xh_ref, x_ref, g_ref, win_ref, cw_ref, cb_ref, wa_ref, ba_ref, wx_ref, bx_ref, lam_ref,
                wout_ref, o_ref, xn_ref, gb_ref, rec_ref, a_ref, b_ref, carry_ref, *, ts, width):
    _normed_with_halo(xh_ref, x_ref, g_ref[...], xn_ref)
    gb_ref[...] = _dot(xn_ref[HALO:, :], win_ref[:, :width])
    rec_ref[...] = _dot(xn_ref[...], win_ref[:, width:])
    block = width // LRU_BLOCKS
    for n in range(LRU_BLOCKS):
        lanes = slice(n * block, (n + 1) * block)
        rec = cb_ref[:, lanes]
        for k in range(LRU_CONV):
            off = HALO - (LRU_CONV - 1) + k
            rec = rec + cw_ref[k:k + 1, lanes] * rec_ref[off:off + ts, lanes]
        rb = rec.astype(BF16)
        r = _sigmoid(_dot(rb, wa_ref[n]) + ba_ref[:, lanes])
        i = _sigmoid(_dot(rb, wx_ref[n]) + bx_ref[:, lanes])
        log_a = (-LRU_C * r) * _softplus(-lam_ref[:, lanes])
        a_ref[:, lanes] = jnp.exp(log_a)
        th = jnp.tanh(log_a)
        b_ref[:, lanes] = jnp.sqrt(-2.0 * th / (1.0 - th)) * (i * rec)

    @pl.when(pl.program_id(1) == 0)
    def _():
        carry_ref[...] = jnp.zeros_like(carry_ref)

    row = lax.broadcasted_iota(jnp.int32, (8, width), 0)

    def body(i, carry):
        r0 = pl.multiple_of(i * 8, 8)
        a = a_ref[pl.ds(r0, 8), :]
        b = b_ref[pl.ds(r0, 8), :]
        for shift in (1, 2, 4):
            has_partner = row >= shift
            a_prev = jnp.where(has_partner, pltpu.roll(a, shift, 0), 1.0)
            b_prev = jnp.where(has_partner, pltpu.roll(b, shift, 0), 0.0)
            b = a * b_prev + b
            a = a * a_prev
        h = a * carry + b
        b_ref[pl.ds(r0, 8), :] = h
        return jnp.broadcast_to(h[7:8, :], (8, width))

    carry_ref[...] = lax.fori_loop(0, ts // 8, body, carry_ref[...], unroll=2)
    y = (_gelu(gb_ref[...]) * b_ref[...]).astype(BF16)
    o_ref[0] = x_ref[0] + _dot(y, wout_ref[...])


def _lru_layer(x, g, w_in, conv_w, conv_b, w_a, b_a, w_x, b_x, lam, w_out, *, ts):
    bsz, seq, d = x.shape
    width = w_out.shape[0]
    halo, tile = _tile_specs(ts, d)
    row = lambda v: v.reshape(1, -1)
    return pl.pallas_call(
        functools.partial(_lru_kernel, ts=ts, width=width),
        grid=(bsz, seq // ts),
        in_specs=[halo, tile, _const_spec((1, d)), _const_spec((d, 2 * width)),
                  _const_spec((LRU_CONV, width)), _const_spec((1, width)),
                  _const_spec(w_a.shape), _const_spec((1, width)),
                  _const_spec(w_x.shape), _const_spec((1, width)), _const_spec((1, width)),
                  _const_spec((width, d))],
        out_specs=tile,
        out_shape=jax.ShapeDtypeStruct(x.shape, x.dtype),
        scratch_shapes=[pltpu.VMEM((HALO + ts, d), BF16),
                        pltpu.VMEM((ts, width), F32),
                        pltpu.VMEM((HALO + ts, width), F32),
                        pltpu.VMEM((ts, width), F32), pltpu.VMEM((ts, width), F32),
                        pltpu.VMEM((8, width), F32)],
        compiler_params=_cparams(),
    )(x, x, row(g), w_in.astype(BF16), conv_w, row(conv_b), w_a.astype(BF16), row(b_a),
      w_x.astype(BF16), row(b_x), row(lam), w_out.astype(BF16))


def _qkv_kernel(x_ref, g_ref, w_ref, qg_ref, kg_ref, gsum_ref, gexp_ref, q_ref, k_ref, v_ref, *, d):
    xn = _rms(x_ref[0], g_ref[...]).astype(BF16)

    def head_normed(t, gain):
        hi, lo = _split(t * t)
        ms = _dot(hi, gsum_ref[...]) + _dot(lo, gsum_ref[...])
        hi, lo = _split(lax.rsqrt(ms + EPS))
        return t * (_dot(hi, gexp_ref[...]) + _dot(lo, gexp_ref[...])) * gain

    scale = LOG2E / math.sqrt(SB_HEAD_DIM)
    q_ref[0] = (head_normed(_dot(xn, w_ref[:, :d]), qg_ref[...]) * scale).astype(BF16)
    k_ref[0] = head_normed(_dot(xn, w_ref[:, d:2 * d]), kg_ref[...]).astype(BF16)
    v_ref[0] = _dot(xn, w_ref[:, 2 * d:]).astype(BF16)


def _qkv_proj(x, g, w_qkv, q_g, k_g, *, ts):
    bsz, seq, d = x.shape
    heads = d // SB_HEAD_DIM
    head_of_lane = jnp.arange(d) // SB_HEAD_DIM
    onehot = (head_of_lane[:, None] == jnp.arange(128)[None, :])
    gsum = (onehot.astype(F32) / SB_HEAD_DIM).astype(BF16)
    gexp = onehot.T.astype(BF16)
    tile = pl.BlockSpec((1, ts, d), lambda b, s: (b, s, 0))
    out = jax.ShapeDtypeStruct(x.shape, BF16)
    return pl.pallas_call(
        functools.partial(_qkv_kernel, d=d),
        grid=(bsz, seq // ts),
        in_specs=[tile, _const_spec((1, d)), _const_spec((d, 3 * d)), _const_spec((1, d)),
                  _const_spec((1, d)), _const_spec((d, 128)), _const_spec((128, d))],
        out_specs=[tile, tile, tile],
        out_shape=[out, out, out],
        compiler_params=_cparams(),
    )(x, g.reshape(1, d), w_qkv.astype(BF16), jnp.tile(q_g, heads).reshape(1, d),
      jnp.tile(k_g, heads).reshape(1, d), gsum, gexp)


SB_TILE = 256
SB_SUB = 128
SB_PAIRS = 4


def _attn_kernel(q_ref, k_ref, v_ref, tri_ref, o_ref, acc_ref, run_ref):
    tq, sub, pair = SB_TILE, SB_SUB, 2 * SB_HEAD_DIM
    qi = pl.program_id(2)
    lane = lax.broadcasted_iota(jnp.int32, (tq, pair), 1)
    first_head = lane < SB_HEAD_DIM
    t_pos = lax.broadcasted_iota(jnp.int32, (tq, tq), 0)
    s_pos = lax.broadcasted_iota(jnp.int32, (tq, tq), 1)
    causal = s_pos < t_pos
    q_heads = []
    for p in range(SB_PAIRS):
        q = q_ref[0, :, p * pair:(p + 1) * pair]
        zero = jnp.zeros_like(q)
        q_heads += [jnp.where(first_head, q, zero), jnp.where(first_head, zero, q)]

    def block(kb, diagonal):
        k0 = pl.multiple_of(kb * tq, tq)
        heads = range(2 * SB_PAIRS)
        kblk = [k_ref[0, pl.ds(k0, tq), p * pair:(p + 1) * pair] for p in range(SB_PAIRS)]
        vblk = [v_ref[0, pl.ds(k0, tq), p * pair:(p + 1) * pair] for p in range(SB_PAIRS)]
        z = [lax.dot_general(q_heads[h], kblk[h // 2], (((1,), (1,)), ((), ())), preferred_element_type=F32)
             for h in heads]
        halves = []
        for h in heads:
            neg_abs = lax.bitcast_convert_type(
                lax.bitcast_convert_type(z[h], jnp.uint32) | jnp.uint32(0x80000000), F32)
            sp = jnp.maximum(z[h], 0.0) + jnp.log(1.0 + jnp.exp2(neg_abs)) * LOG2E
            if diagonal:
                sp = jnp.where(causal, sp, 0.0)
            halves.append(_split(sp))
        sums = [[_dot(jnp.concatenate([hi[:, c * sub:(c + 1) * sub], lo[:, c * sub:(c + 1) * sub]], axis=1),
                      tri_ref[...]) for c in range(tq // sub)] for hi, lo in halves]
        atts = []
        for h in heads:
            run = run_ref[h]
            parts = []
            for c in reversed(range(tq // sub)):
                parts.append(jnp.exp2(z[h][:, c * sub:(c + 1) * sub] + sums[h][c][:, :sub] + run))
                run = run + sums[h][c][:, sub:]
            run_ref[h] = run
            att = jnp.concatenate(parts[::-1], axis=1)
            if diagonal:
                att = jnp.where(causal, att, 0.0)
            atts.append(att.astype(BF16))
        for h in heads:
            acc_ref[h] += _dot(atts[h], vblk[h // 2])

    acc_ref[...] = jnp.zeros_like(acc_ref)
    run_ref[...] = jnp.zeros_like(run_ref)
    block(qi, True)

    def earlier(j, carry):
        block(qi - 1 - j, False)
        return carry

    lax.fori_loop(0, qi, earlier, 0)
    for p in range(SB_PAIRS):
        o_ref[0, :, p * pair:(p + 1) * pair] = jnp.where(
            first_head, acc_ref[2 * p], acc_ref[2 * p + 1]).astype(BF16)


def _attention(q, k, v):
    bsz, seq, d = q.shape
    tq, sub = SB_TILE, SB_SUB
    lanes = SB_PAIRS * 2 * SB_HEAD_DIM
    j = jnp.arange(2 * sub)[:, None] % sub
    s = jnp.arange(2 * sub)[None, :]
    tri = -((s >= sub) | (j >= s)).astype(BF16)
    qspec = pl.BlockSpec((1, tq, lanes), lambda b, h, i: (b, i, h))
    kvspec = pl.BlockSpec((1, seq, lanes), lambda b, h, i: (b, 0, h))
    return pl.pallas_call(
        _attn_kernel,
        grid=(bsz, d // lanes, seq // tq),
        in_specs=[qspec, kvspec, kvspec, pl.BlockSpec((2 * sub, 2 * sub), lambda b, h, i: (0, 0))],
        out_specs=qspec,
        out_shape=jax.ShapeDtypeStruct(q.shape, BF16),
        scratch_shapes=[pltpu.VMEM((2 * SB_PAIRS, tq, 2 * SB_HEAD_DIM), F32),
                        pltpu.VMEM((2 * SB_PAIRS, tq, sub), F32)],
        compiler_params=pltpu.CompilerParams(
            dimension_semantics=("arbitrary", "arbitrary", "arbitrary"), vmem_limit_bytes=VMEM_LIMIT),
    )(q, k, v, tri)


def _proj_kernel(x_ref, a_ref, w_ref, o_ref):
    o_ref[0] = x_ref[0] + _dot(a_ref[0], w_ref[...])


def _residual_proj(x, a, w, *, ts):
    bsz, seq, d = x.shape
    tile = pl.BlockSpec((1, ts, d), lambda b, s: (b, s, 0))
    atile = pl.BlockSpec((1, ts, a.shape[-1]), lambda b, s: (b, s, 0))
    return pl.pallas_call(
        _proj_kernel,
        grid=(bsz, seq // ts),
        in_specs=[tile, atile, _const_spec(w.shape)],
        out_specs=tile,
        out_shape=jax.ShapeDtypeStruct(x.shape, x.dtype),
        compiler_params=_cparams(),
    )(x, a, w.astype(BF16))


def _sb_layer(x, g, w_qkv, q_g, k_g, w_o, *, ts):
    q, k, v = _qkv_proj(x, g, w_qkv, q_g, k_g, ts=ts)
    return _residual_proj(x, _attention(q, k, v), w_o, ts=ts)


def _tile_rows(seq, want):
    ts = min(seq, want)
    assert seq % ts == 0 and ts % HALO == 0, (seq, ts)
    return ts


def kernel(x, norm_mix_g, norm_ffn_g, pool_w, pool_b, pool_scale, s5_lam_re, s5_lam_im, s5_log_dt, s5_b_re, s5_b_im, s5_c_re, s5_c_im, s5_d, s5_w_out, s5_b_out, lru_w_in, lru_conv_w, lru_conv_b, lru_w_a, lru_b_a, lru_w_x, lru_b_x, lru_lam, lru_w_out, sb_w_qkv, sb_q_g, sb_k_g, sb_w_o, ffn_w_in, ffn_conv_w, ffn_conv_b, ffn_w_out):
    depth = norm_mix_g.shape[0]
    seq = x.shape[1]
    for layer in range(depth):
        m, j = layer % 4, layer // 4
        g = norm_mix_g[layer]
        if m == 0:
            x = _pool_layer(x, g, pool_w[j], pool_b[j], pool_scale[j], ts=_tile_rows(seq, 512))
        elif m == 1:
            x = _s5_layer(x, g, s5_lam_re[j], s5_lam_im[j], s5_log_dt[j], s5_b_re[j], s5_b_im[j],
                          s5_c_re[j], s5_c_im[j], s5_d[j], s5_w_out[j], s5_b_out[j],
                          ts=_tile_rows(seq, 256))
        elif m == 2:
            x = _lru_layer(x, g, lru_w_in[j], lru_conv_w[j], lru_conv_b[j], lru_w_a[j], lru_b_a[j],
                           lru_w_x[j], lru_b_x[j], lru_lam[j], lru_w_out[j], ts=_tile_rows(seq, 512))
        else:
            x = _sb_layer(x, g, sb_w_qkv[j], sb_q_g[j], sb_k_g[j], sb_w_o[j], ts=_tile_rows(seq, 512))
        x = _conv_ffn(x, norm_ffn_g[layer], ffn_w_in[layer], ffn_conv_w[layer], ffn_conv_b[layer],
                      ffn_w_out[layer], ts=_tile_rows(seq, 512))
    return x
```

```python
import functools
import math

import jax
import jax.numpy as jnp
from jax import lax
from jax.experimental import pallas as pl
from jax.experimental.pallas import tpu as pltpu

F32 = jnp.float32
BF16 = jnp.bfloat16

EPS = 1e-6
LOG2E = 1.0 / math.log(2.0)
HALO = 16
POOL_WINDOWS = (2, 4, 8, 16)
S5_GROUP = 16
S5_STATE = 64
S5_SLAB = 128
LRU_BLOCKS = 4
LRU_CONV = 4
LRU_C = 8.0
SB_HEAD_DIM = 64
FFN_CONV = 3
FFN_CHUNK = 256
VMEM_LIMIT = 56 * 1024 * 1024


def _cparams():
    return pltpu.CompilerParams(
        dimension_semantics=("arbitrary", "arbitrary"),
        vmem_limit_bytes=VMEM_LIMIT)


def _rms(x, g):
    ms = jnp.mean(x * x, axis=-1, keepdims=True)
    return x * lax.rsqrt(ms + EPS) * g


def _gelu(x):
    c = math.sqrt(2.0 / math.pi)
    return 0.5 * x * (1.0 + jnp.tanh(c * (x + 0.044715 * (x * x * x))))


def _sigmoid(x):
    return 1.0 / (1.0 + jnp.exp(-x))


def _softplus(x):
    return jnp.maximum(x, 0.0) + jnp.log(1.0 + jnp.exp(-jnp.abs(x)))


def _dot(a, b):
    return jnp.dot(a, b, preferred_element_type=F32)


def _split(x):
    hi = x.astype(BF16)
    lo = (x - hi.astype(F32)).astype(BF16)
    return hi, lo


def _normed_with_halo(xh_ref, x_ref, g, xn_ref):
    first = pl.program_id(1) == 0
    hn = _rms(xh_ref[0], g)
    xn_ref[0:HALO, :] = jnp.where(first, 0.0, hn).astype(xn_ref.dtype)
    xn_ref[HALO:, :] = _rms(x_ref[0], g).astype(xn_ref.dtype)


def _tile_specs(ts, d):
    per = ts // HALO
    halo = pl.BlockSpec((1, HALO, d), lambda b, s: (b, jnp.maximum(s * per - 1, 0), 0))
    tile = pl.BlockSpec((1, ts, d), lambda b, s: (b, s, 0))
    return halo, tile


def _const_spec(shape):
    zeros = (0,) * len(shape)
    return pl.BlockSpec(shape, lambda b, s: zeros, pipeline_mode=pl.Buffered(1))


def _ffn_kernel(xh_ref, x_ref, g_ref, win_ref, cw_ref, cb_ref, wout_ref, o_ref, xn_ref, a_ref, *, ts, hidden):
    _normed_with_halo(xh_ref, x_ref, g_ref[...], xn_ref)
    assert FFN_CONV == 3
    for c in range(hidden // FFN_CHUNK):
        conv = []
        for part in range(2):
            cols = slice(part * hidden + c * FFN_CHUNK, part * hidden + (c + 1) * FFN_CHUNK)
            h = _dot(xn_ref[...], win_ref[:, cols])
            inner = cw_ref[1:2, cols] * h + pltpu.roll(cw_ref[0:1, cols] * h, 1, 0)
            y = (cw_ref[2:3, cols] * h + cb_ref[:, cols]) + pltpu.roll(inner, 1, 0)
            conv.append(y[HALO:, :])
        val, gate = conv
        half = 0.5 * gate
        a_ref[:, c * FFN_CHUNK:(c + 1) * FFN_CHUNK] = ((half + half * jnp.tanh(half)) * val).astype(BF16)
    o_ref[0] = x_ref[0] + _dot(a_ref[...], wout_ref[...])


def _conv_ffn(x, g, w_in, conv_w, conv_b, w_out, *, ts):
    bsz, seq, d = x.shape
    hidden = w_out.shape[0]
    halo, tile = _tile_specs(ts, d)
    return pl.pallas_call(
        functools.partial(_ffn_kernel, ts=ts, hidden=hidden),
        grid=(bsz, seq // ts),
        in_specs=[halo, tile, _const_spec((1, d)), _const_spec((d, 2 * hidden)),
                  _const_spec((FFN_CONV, 2 * hidden)), _const_spec((1, 2 * hidden)),
                  _const_spec((hidden, d))],
        out_specs=tile,
        out_shape=jax.ShapeDtypeStruct(x.shape, x.dtype),
        scratch_shapes=[pltpu.VMEM((HALO + ts, d), BF16),
                        pltpu.VMEM((ts, hidden), BF16)],
        compiler_params=_cparams(),
    )(x, x, g.reshape(1, d), w_in.astype(BF16), conv_w, conv_b.reshape(1, -1), w_out.astype(BF16))


def _pool_kernel(xh_ref, x_ref, g_ref, w_ref, b_ref, sc_ref, o_ref, hn_ref, *, ts, group):
    _normed_with_halo(xh_ref, x_ref, g_ref[...], hn_ref)
    pos = pl.program_id(1) * ts + lax.broadcasted_iota(jnp.int32, (ts, 1), 0)
    for gi, win in enumerate(POOL_WINDOWS):
        lanes = slice(gi * group, (gi + 1) * group)
        acc = hn_ref[:, lanes]
        shift = 1
        while shift < win:
            acc = acc + pltpu.roll(acc, shift, 0)
            shift *= 2
        h = hn_ref[HALO:HALO + ts, lanes]
        count = jnp.minimum(pos + 1, win).astype(F32)
        diff = acc[HALO:, :] / count - h
        y = _dot(diff.astype(BF16), w_ref[gi]) + b_ref[:, lanes]
        o_ref[0, :, lanes] = x_ref[0, :, lanes] + sc_ref[:, lanes] * y


def _pool_layer(x, g, w, b, scale, *, ts):
    bsz, seq, d = x.shape
    ngroup, group = w.shape[0], w.shape[1]
    halo, tile = _tile_specs(ts, d)
    return pl.pallas_call(
        functools.partial(_pool_kernel, ts=ts, group=group),
        grid=(bsz, seq // ts),
        in_specs=[halo, tile, _const_spec((1, d)), _const_spec((ngroup, group, group)),
                  _const_spec((1, d)), _const_spec((1, d))],
        out_specs=tile,
        out_shape=jax.ShapeDtypeStruct(x.shape, x.dtype),
        scratch_shapes=[pltpu.VMEM((HALO + ts, d), F32)],
        compiler_params=_cparams(),
    )(x, x, g.reshape(1, d), w.astype(BF16), b.reshape(1, d), scale.reshape(1, d))


S5_CHUNK = 8
S5_LANE_CHUNK = 256


def _s5_norm_kernel(x_ref, g_ref, u_ref):
    u_ref[0] = _rms(x_ref[0], g_ref[...])


def _s5_core_kernel(u_ref, wz_ref, tz_ref, wexp_ref, pw_ref, dsk_ref, o_ref, xc_ref, z_ref, y_ref, *, seq):
    m = seq // S5_CHUNK
    sl = S5_SLAB
    for r in range(S5_CHUNK):
        xc_ref[:, r * sl:(r + 1) * sl] = u_ref[0, pl.ds(r, m, stride=S5_CHUNK), :].astype(BF16)
    half = z_ref.shape[1] // 2
    z_ref[0:8, :] = jnp.zeros((8, 2 * half), F32)
    z_ref[8:, :] = _dot(xc_ref[...], wz_ref[0])

    lc = S5_LANE_CHUNK
    for c in range(half // lc):
        re = slice(c * lc, (c + 1) * lc)
        im = slice(half + c * lc, half + (c + 1) * lc)
        pws = [(pw_ref[0, i, 0, :, re], pw_ref[0, i, 1, :, re]) for i in range(4)]

        def body(i, carry, re=re, im=im, pws=pws):
            c_re, c_im = carry
            r0 = pl.multiple_of(8 + i * 8, 8)
            b_re = z_ref[pl.ds(r0, 8), re]
            b_im = z_ref[pl.ds(r0, 8), im]
            for step, shift in enumerate((1, 2, 4)):
                p_re, p_im = pws[step]
                s_re = pltpu.roll(b_re, shift, 0)
                s_im = pltpu.roll(b_im, shift, 0)
                b_re, b_im = (b_re + (p_re * s_re - p_im * s_im),
                              b_im + (p_re * s_im + p_im * s_re))
            p_re, p_im = pws[3]
            h_re = b_re + (p_re * c_re - p_im * c_im)
            h_im = b_im + (p_re * c_im + p_im * c_re)
            z_ref[pl.ds(r0, 8), re] = h_re
            z_ref[pl.ds(r0, 8), im] = h_im
            return (jnp.broadcast_to(h_re[7:8, :], (8, lc)), jnp.broadcast_to(h_im[7:8, :], (8, lc)))

        zero = jnp.zeros((8, lc), F32)
        lax.fori_loop(0, m // 8, body, (zero, zero), unroll=2)

    before = z_ref[7:7 + m, :].astype(BF16)
    yc = _dot(xc_ref[...], tz_ref[0]) + _dot(before, wexp_ref[0])
    for r in range(S5_CHUNK):
        rows = pl.ds(r, m, stride=S5_CHUNK)
        y_ref[rows, :] = _gelu(yc[:, r * sl:(r + 1) * sl] + dsk_ref[...] * u_ref[0, rows, :])
    o_ref[0] = y_ref[...].astype(BF16)


def _s5_out_kernel(x_ref, y_ref, w_ref, b_ref, o_ref, *, d):
    z = _dot(y_ref[0], w_ref[...]) + b_ref[...]
    o_ref[0] = x_ref[0] + z[:, :d] * _sigmoid(z[:, d:])


def _s5_expand_kernel(a_ref, e_ref, o_ref, *, row_div, col_div, per):
    rows, cols = o_ref.shape[1:]
    full = _dot(a_ref[0], e_ref[...])
    row_group = (lax.broadcasted_iota(jnp.int32, (rows, cols), 0) >> (row_div.bit_length() - 1)) & (per - 1)
    col_group = (lax.broadcasted_iota(jnp.int32, (rows, cols), 1) >> (col_div.bit_length() - 1)) & (per - 1)
    o_ref[0] = jnp.where(row_group == col_group, full, 0.0).astype(BF16)


def _s5_expand(compact, src_of_col, *, row_div, col_div, per):
    nslab, rows, k = compact.shape
    cols = src_of_col.shape[0]
    select = (jnp.arange(k)[:, None] == src_of_col[None, :]).astype(BF16)
    return pl.pallas_call(
        functools.partial(_s5_expand_kernel, row_div=row_div, col_div=col_div, per=per),
        grid=(nslab,),
        in_specs=[pl.BlockSpec((1, rows, k), lambda j: (j, 0, 0)), pl.BlockSpec((k, cols), lambda j: (0, 0))],
        out_specs=pl.BlockSpec((1, rows, cols), lambda j: (j, 0, 0)),
        out_shape=jax.ShapeDtypeStruct((nslab, rows, cols), BF16),
        compiler_params=pltpu.CompilerParams(dimension_semantics=("arbitrary",), vmem_limit_bytes=VMEM_LIMIT),
    )(compact.astype(BF16), select)


def _s5_tables(lam_re, lam_im, log_dt, b_re, b_im, c_re, c_im, d):
    ngroups, nstate = lam_re.shape
    per = S5_SLAB // S5_GROUP
    nslab = d // S5_SLAB
    nl = S5_CHUNK
    hh = S5_GROUP
    lam = lax.complex(jnp.minimum(lam_re.astype(F32), -1e-4), lam_im.astype(F32))
    dt = jnp.exp(log_dt.astype(F32))[:, None]
    lam_bar = jnp.exp(lam * dt)
    b_bar = ((lam_bar - 1.0) / lam)[..., None] * lax.complex(b_re.astype(F32), b_im.astype(F32))
    cc = lax.complex(c_re.astype(F32), c_im.astype(F32))
    powers = [jnp.ones_like(lam_bar)]
    for _ in range(nl):
        powers.append(powers[-1] * lam_bar)
    powers = jnp.stack(powers)
    rows = nl * S5_SLAB
    col = jnp.arange(rows)

    t = powers[nl - 1::-1][:nl, :, :, None] * b_bar[None]
    t = t.reshape(nl, nslab, per, nstate, hh).transpose(1, 0, 2, 4, 3)
    wz = _s5_expand(jnp.stack([jnp.real(t), jnp.imag(t)], axis=4).reshape(nslab, rows, 2 * nstate),
                    (col // (per * nstate)) * nstate + col % nstate, row_div=hh, col_div=nstate, per=per)

    kern = jnp.real(jnp.einsum('gip,kgp,gpj->gkij', cc, powers[:nl], b_bar))
    lag = jnp.arange(nl)[None, :] - jnp.arange(nl)[:, None]
    kl = jnp.where((lag >= 0)[None, :, :, None, None], kern[:, jnp.clip(lag, 0, nl - 1)], 0.0)
    kl = kl.reshape(nslab, per, nl, nl, hh, hh).transpose(0, 2, 1, 5, 3, 4)
    step_of_col = (col // S5_SLAB) * hh + col % hh
    tz = _s5_expand(kl.reshape(nslab, rows, nl * hh), step_of_col, row_div=hh, col_div=hh, per=per)

    e = cc[None] * powers[1:nl + 1, :, None, :]
    e = e.reshape(nl, nslab, per, hh, nstate).transpose(1, 2, 4, 0, 3)
    wexp = _s5_expand(jnp.stack([jnp.real(e), -jnp.imag(e)], axis=1).reshape(nslab, 2 * per * nstate, nl * hh),
                      step_of_col, row_div=nstate, col_div=hh, per=per)

    step1 = powers[nl].reshape(nslab, 1, per * nstate)
    step2 = step1 * step1
    step4 = step2 * step2
    row = jnp.arange(8)[None, :, None]
    tabs = [jnp.where(row >= s, p, 0.0) for s, p in ((1, step1), (2, step2), (4, step4))]
    ramp = [step1]
    for _ in range(7):
        ramp.append(ramp[-1] * step1)
    tabs.append(jnp.concatenate(ramp, axis=1))
    pw = jnp.stack(tabs, axis=1)
    pw = jnp.stack([jnp.real(pw), jnp.imag(pw)], axis=2).astype(F32)
    return wz, tz, wexp, pw


def _s5_layer(x, g, lam_re, lam_im, log_dt, b_re, b_im, c_re, c_im, d_skip, w_out, b_out, *, ts):
    bsz, seq, d = x.shape
    assert seq % (8 * S5_CHUNK) == 0, seq
    wz, tz, wexp, pw = _s5_tables(lam_re, lam_im, log_dt, b_re, b_im, c_re, c_im, d)
    nslab = d // S5_SLAB
    tile = pl.BlockSpec((1, ts, d), lambda b, s: (b, s, 0))
    u = pl.pallas_call(
        _s5_norm_kernel,
        grid=(bsz, seq // ts),
        in_specs=[tile, _const_spec((1, d))],
        out_specs=tile,
        out_shape=jax.ShapeDtypeStruct(x.shape, F32),
        compiler_params=_cparams(),
    )(x, g.reshape(1, d))

    slab = pl.BlockSpec((1, seq, S5_SLAB), lambda j, b: (b, 0, j))
    m = seq // S5_CHUNK
    cols = pw.shape[-1]

    def per_slab(shape):
        zeros = (0,) * (len(shape) - 1)
        return pl.BlockSpec((1,) + shape[1:], lambda j, b: (j,) + zeros, pipeline_mode=pl.Buffered(1))

    y = pl.pallas_call(
        functools.partial(_s5_core_kernel, seq=seq),
        grid=(nslab, bsz),
        in_specs=[slab, per_slab(wz.shape), per_slab(tz.shape), per_slab(wexp.shape), per_slab(pw.shape),
                  pl.BlockSpec((1, S5_SLAB), lambda j, b: (0, j))],
        out_specs=slab,
        out_shape=jax.ShapeDtypeStruct(x.shape, BF16),
        scratch_shapes=[pltpu.VMEM((m, S5_CHUNK * S5_SLAB), BF16),
                        pltpu.VMEM((8 + m, 2 * cols), F32),
                        pltpu.VMEM((seq, S5_SLAB), F32)],
        compiler_params=_cparams(),
    )(u, wz, tz, wexp, pw, d_skip.reshape(1, d))

    return pl.pallas_call(
        functools.partial(_s5_out_kernel, d=d),
        grid=(bsz, seq // ts),
        in_specs=[tile, tile, _const_spec((d, 2 * d)), _const_spec((1, 2 * d))],
        out_specs=tile,
        out_shape=jax.ShapeDtypeStruct(x.shape, x.dtype),
        compiler_params=_cparams(),
    )(x, y, w_out.astype(BF16), b_out.reshape(1, 2 * d))


def _lru_kernel(xh_ref, x_ref, g_ref, win_ref, cw_ref, cb_ref, wa_ref, ba_ref, wx_ref, bx_ref, lam_ref,
                wout_ref, o_ref, xn_ref, gb_ref, rec_ref, a_ref, b_ref, carry_ref, *, ts, width):
    _normed_with_halo(xh_ref, x_ref, g_ref[...], xn_ref)
    gb_ref[...] = _dot(xn_ref[HALO:, :], win_ref[:, :width])
    rec_ref[...] = _dot(xn_ref[...], win_ref[:, width:])
    block = width // LRU_BLOCKS
    for n in range(LRU_BLOCKS):
        lanes = slice(n * block, (n + 1) * block)
        h = rec_ref[:, lanes]
        acc = cw_ref[0:1, lanes] * h
        for k in range(1, LRU_CONV):
            acc = cw_ref[k:k + 1, lanes] * h + pltpu.roll(acc, 1, 0)
        rec = acc[HALO:, :] + cb_ref[:, lanes]
        rb = rec.astype(BF16)
        r = _sigmoid(_dot(rb, wa_ref[n]) + ba_ref[:, lanes])
        i = _sigmoid(_dot(rb, wx_ref[n]) + bx_ref[:, lanes])
        log_a = (-LRU_C * r) * _softplus(-lam_ref[:, lanes])
        a_ref[:, lanes] = jnp.exp(log_a)
        th = jnp.tanh(log_a)
        b_ref[:, lanes] = jnp.sqrt(-2.0 * th / (1.0 - th)) * (i * rec)

    @pl.when(pl.program_id(1) == 0)
    def _():
        carry_ref[...] = jnp.zeros_like(carry_ref)

    row = lax.broadcasted_iota(jnp.int32, (8, width), 0)

    def body(i, carry):
        r0 = pl.multiple_of(i * 8, 8)
        a = a_ref[pl.ds(r0, 8), :]
        b = b_ref[pl.ds(r0, 8), :]
        for shift in (1, 2, 4):
            has_partner = row >= shift
            a_prev = jnp.where(has_partner, pltpu.roll(a, shift, 0), 1.0)
            b_prev = jnp.where(has_partner, pltpu.roll(b, shift, 0), 0.0)
            b = a * b_prev + b
            a = a * a_prev
        h = a * carry + b
        b_ref[pl.ds(r0, 8), :] = h
        return jnp.broadcast_to(h[7:8, :], (8, width))

    carry_ref[...] = lax.fori_loop(0, ts // 8, body, carry_ref[...], unroll=2)
    y = (_gelu(gb_ref[...]) * b_ref[...]).astype(BF16)
    o_ref[0] = x_ref[0] + _dot(y, wout_ref[...])


def _lru_layer(x, g, w_in, conv_w, conv_b, w_a, b_a, w_x, b_x, lam, w_out, *, ts):
    bsz, seq, d = x.shape
    width = w_out.shape[0]
    halo, tile = _tile_specs(ts, d)
    row = lambda v: v.reshape(1, -1)
    return pl.pallas_call(
        functools.partial(_lru_kernel, ts=ts, width=width),
        grid=(bsz, seq // ts),
        in_specs=[halo, tile, _const_spec((1, d)), _const_spec((d, 2 * width)),
                  _const_spec((LRU_CONV, width)), _const_spec((1, width)),
                  _const_spec(w_a.shape), _const_spec((1, width)),
                  _const_spec(w_x.shape), _const_spec((1, width)), _const_spec((1, width)),
                  _const_spec((width, d))],
        out_specs=tile,
        out_shape=jax.ShapeDtypeStruct(x.shape, x.dtype),
        scratch_shapes=[pltpu.VMEM((HALO + ts, d), BF16),
                        pltpu.VMEM((ts, width), F32),
                        pltpu.VMEM((HALO + ts, width), F32),
                        pltpu.VMEM((ts, width), F32), pltpu.VMEM((ts, width), F32),
                        pltpu.VMEM((8, width), F32)],
        compiler_params=_cparams(),
    )(x, x, row(g), w_in.astype(BF16), conv_w, row(conv_b), w_a.astype(BF16), row(b_a),
      w_x.astype(BF16), row(b_x), row(lam), w_out.astype(BF16))


def _qkv_kernel(x_ref, g_ref, w_ref, qg_ref, kg_ref, gsum_ref, gexp_ref, q_ref, k_ref, v_ref, *, d):
    xn = _rms(x_ref[0], g_ref[...]).astype(BF16)

    def head_normed(t, gain):
        hi, lo = _split(t * t)
        ms = _dot(hi, gsum_ref[...]) + _dot(lo, gsum_ref[...])
        hi, lo = _split(lax.rsqrt(ms + EPS))
        return t * (_dot(hi, gexp_ref[...]) + _dot(lo, gexp_ref[...])) * gain

    scale = LOG2E / math.sqrt(SB_HEAD_DIM)
    q_ref[0] = (head_normed(_dot(xn, w_ref[:, :d]), qg_ref[...]) * scale).astype(BF16)
    k_ref[0] = head_normed(_dot(xn, w_ref[:, d:2 * d]), kg_ref[...]).astype(BF16)
    v_ref[0] = _dot(xn, w_ref[:, 2 * d:]).astype(BF16)


def _qkv_proj(x, g, w_qkv, q_g, k_g, *, ts):
    bsz, seq, d = x.shape
    heads = d // SB_HEAD_DIM
    head_of_lane = jnp.arange(d) // SB_HEAD_DIM
    onehot = (head_of_lane[:, None] == jnp.arange(128)[None, :])
    gsum = (onehot.astype(F32) / SB_HEAD_DIM).astype(BF16)
    gexp = onehot.T.astype(BF16)
    tile = pl.BlockSpec((1, ts, d), lambda b, s: (b, s, 0))
    out = jax.ShapeDtypeStruct(x.shape, BF16)
    return pl.pallas_call(
        functools.partial(_qkv_kernel, d=d),
        grid=(bsz, seq // ts),
        in_specs=[tile, _const_spec((1, d)), _const_spec((d, 3 * d)), _const_spec((1, d)),
                  _const_spec((1, d)), _const_spec((d, 128)), _const_spec((128, d))],
        out_specs=[tile, tile, tile],
        out_shape=[out, out, out],
        compiler_params=_cparams(),
    )(x, g.reshape(1, d), w_qkv.astype(BF16), jnp.tile(q_g, heads).reshape(1, d),
      jnp.tile(k_g, heads).reshape(1, d), gsum, gexp)


SB_TILE = 256
SB_SUB = 128
SB_PAIRS = 4


def _attn_kernel(q_ref, k_ref, v_ref, tri_ref, o_ref, acc_ref, run_ref):
    tq, sub, pair = SB_TILE, SB_SUB, 2 * SB_HEAD_DIM
    qi = pl.program_id(2)
    lane = lax.broadcasted_iota(jnp.int32, (tq, pair), 1)
    first_head = lane < SB_HEAD_DIM
    t_pos = lax.broadcasted_iota(jnp.int32, (tq, tq), 0)
    s_pos = lax.broadcasted_iota(jnp.int32, (tq, tq), 1)
    causal = s_pos < t_pos
    q_heads = []
    for p in range(SB_PAIRS):
        q = q_ref[0, :, p * pair:(p + 1) * pair]
        zero = jnp.zeros_like(q)
        q_heads += [jnp.where(first_head, q, zero), jnp.where(first_head, zero, q)]

    def block(kb, diagonal):
        k0 = pl.multiple_of(kb * tq, tq)
        heads = range(2 * SB_PAIRS)
        kblk = [k_ref[0, pl.ds(k0, tq), p * pair:(p + 1) * pair] for p in range(SB_PAIRS)]
        vblk = [v_ref[0, pl.ds(k0, tq), p * pair:(p + 1) * pair] for p in range(SB_PAIRS)]
        z = [lax.dot_general(q_heads[h], kblk[h // 2], (((1,), (1,)), ((), ())), preferred_element_type=F32)
             for h in heads]
        halves = []
        for h in heads:
            neg_abs = lax.bitcast_convert_type(
                lax.bitcast_convert_type(z[h], jnp.uint32) | jnp.uint32(0x80000000), F32)
            sp = jnp.maximum(z[h], 0.0) + jnp.log(1.0 + jnp.exp2(neg_abs)) * LOG2E
            if diagonal:
                sp = jnp.where(causal, sp, 0.0)
            halves.append(_split(sp))
        sums = [[_dot(jnp.concatenate([hi[:, c * sub:(c + 1) * sub], lo[:, c * sub:(c + 1) * sub]], axis=1),
                      tri_ref[...]) for c in range(tq // sub)] for hi, lo in halves]
        atts = []
        for h in heads:
            run = run_ref[h]
            parts = []
            for c in reversed(range(tq // sub)):
                both = sums[h][c] + jnp.concatenate([run, run], axis=1)
                parts.append(jnp.exp2(z[h][:, c * sub:(c + 1) * sub] + both[:, :sub]))
                run = both[:, sub:]
            run_ref[h] = run
            att = jnp.concatenate(parts[::-1], axis=1)
            if diagonal:
                att = jnp.where(causal, att, 0.0)
            atts.append(att.astype(BF16))
        for h in heads:
            acc_ref[h] += _dot(atts[h], vblk[h // 2])

    acc_ref[...] = jnp.zeros_like(acc_ref)
    run_ref[...] = jnp.zeros_like(run_ref)
    block(qi, True)

    def earlier(j, carry):
        block(qi - 1 - j, False)
        return carry

    lax.fori_loop(0, qi, earlier, 0)
    for p in range(SB_PAIRS):
        o_ref[0, :, p * pair:(p + 1) * pair] = jnp.where(
            first_head, acc_ref[2 * p], acc_ref[2 * p + 1]).astype(BF16)


def _attention(q, k, v):
    bsz, seq, d = q.shape
    tq, sub = SB_TILE, SB_SUB
    lanes = SB_PAIRS * 2 * SB_HEAD_DIM
    j = jnp.arange(2 * sub)[:, None] % sub
    s = jnp.arange(2 * sub)[None, :]
    tri = -((s >= sub) | (j >= s)).astype(BF16)
    qspec = pl.BlockSpec((1, tq, lanes), lambda b, h, i: (b, i, h))
    kvspec = pl.BlockSpec((1, seq, lanes), lambda b, h, i: (b, 0, h))
    return pl.pallas_call(
        _attn_kernel,
        grid=(bsz, d // lanes, seq // tq),
        in_specs=[qspec, kvspec, kvspec, pl.BlockSpec((2 * sub, 2 * sub), lambda b, h, i: (0, 0))],
        out_specs=qspec,
        out_shape=jax.ShapeDtypeStruct(q.shape, BF16),
        scratch_shapes=[pltpu.VMEM((2 * SB_PAIRS, tq, 2 * SB_HEAD_DIM), F32),
                        pltpu.VMEM((2 * SB_PAIRS, tq, sub), F32)],
        compiler_params=pltpu.CompilerParams(
            dimension_semantics=("arbitrary", "arbitrary", "arbitrary"), vmem_limit_bytes=VMEM_LIMIT),
    )(q, k, v, tri)


def _proj_kernel(x_ref, a_ref, w_ref, o_ref):
    o_ref[0] = x_ref[0] + _dot(a_ref[0], w_ref[...])


def _residual_proj(x, a, w, *, ts):
    bsz, seq, d = x.shape
    tile = pl.BlockSpec((1, ts, d), lambda b, s: (b, s, 0))
    atile = pl.BlockSpec((1, ts, a.shape[-1]), lambda b, s: (b, s, 0))
    return pl.pallas_call(
        _proj_kernel,
        grid=(bsz, seq // ts),
        in_specs=[tile, atile, _const_spec(w.shape)],
        out_specs=tile,
        out_shape=jax.ShapeDtypeStruct(x.shape, x.dtype),
        compiler_params=_cparams(),
    )(x, a, w.astype(BF16))


def _sb_layer(x, g, w_qkv, q_g, k_g, w_o, *, ts):
    q, k, v = _qkv_proj(x, g, w_qkv, q_g, k_g, ts=ts)
    return _residual_proj(x, _attention(q, k, v), w_o, ts=ts)


def _tile_rows(seq, want):
    ts = min(seq, want)
    assert seq % ts == 0 and ts % HALO == 0, (seq, ts)
    return ts


def kernel(x, norm_mix_g, norm_ffn_g, pool_w, pool_b, pool_scale, s5_lam_re, s5_lam_im, s5_log_dt, s5_b_re, s5_b_im, s5_c_re, s5_c_im, s5_d, s5_w_out, s5_b_out, lru_w_in, lru_conv_w, lru_conv_b, lru_w_a, lru_b_a, lru_w_x, lru_b_x, lru_lam, lru_w_out, sb_w_qkv, sb_q_g, sb_k_g, sb_w_o, ffn_w_in, ffn_conv_w, ffn_conv_b, ffn_w_out):
    depth = norm_mix_g.shape[0]
    seq = x.shape[1]
    for layer in range(depth):
        m, j = layer % 4, layer // 4
        g = norm_mix_g[layer]
        if m == 0:
            x = _pool_layer(x, g, pool_w[j], pool_b[j], pool_scale[j], ts=_tile_rows(seq, 512))
        elif m == 1:
            x = _s5_layer(x, g, s5_lam_re[j], s5_lam_im[j], s5_log_dt[j], s5_b_re[j], s5_b_im[j],
                          s5_c_re[j], s5_c_im[j], s5_d[j], s5_w_out[j], s5_b_out[j],
                          ts=_tile_rows(seq, 256))
        elif m == 2:
            x = _lru_layer(x, g, lru_w_in[j], lru_conv_w[j], lru_conv_b[j], lru_w_a[j], lru_b_a[j],
                           lru_w_x[j], lru_b_x[j], lru_lam[j], lru_w_out[j], ts=_tile_rows(seq, 512))
        else:
            x = _sb_layer(x, g, sb_w_qkv[j], sb_q_g[j], sb_k_g[j], sb_w_o[j], ts=_tile_rows(seq, 512))
        x = _conv_ffn(x, norm_ffn_g[layer], ffn_w_in[layer], ffn_conv_w[layer], ffn_conv_b[layer],
                      ffn_w_out[layer], ts=_tile_rows(seq, 1024))
    return x
```

```python
import functools
import math

import jax
import jax.numpy as jnp
from jax import lax
from jax.experimental import pallas as pl
from jax.experimental.pallas import tpu as pltpu

F32 = jnp.float32
BF16 = jnp.bfloat16

EPS = 1e-6
LOG2E = 1.0 / math.log(2.0)
HALO = 16
POOL_WINDOWS = (2, 4, 8, 16)
S5_GROUP = 16
S5_STATE = 64
S5_SLAB = 128
LRU_BLOCKS = 4
LRU_CONV = 4
LRU_C = 8.0
SB_HEAD_DIM = 64
FFN_CONV = 3
FFN_CHUNK = 256
VMEM_LIMIT = 56 * 1024 * 1024


def _cparams():
    return pltpu.CompilerParams(
        dimension_semantics=("arbitrary", "arbitrary"),
        vmem_limit_bytes=VMEM_LIMIT)


def _rms(x, g):
    ms = jnp.mean(x * x, axis=-1, keepdims=True)
    return x * lax.rsqrt(ms + EPS) * g


def _gelu(x):
    c = math.sqrt(2.0 / math.pi)
    return 0.5 * x * (1.0 + jnp.tanh(c * (x + 0.044715 * (x * x * x))))


def _sigmoid(x):
    return 1.0 / (1.0 + jnp.exp(-x))


def _softplus(x):
    return jnp.maximum(x, 0.0) + jnp.log(1.0 + jnp.exp(-jnp.abs(x)))


def _dot(a, b):
    return jnp.dot(a, b, preferred_element_type=F32)


def _split(x):
    hi = x.astype(BF16)
    lo = (x - hi.astype(F32)).astype(BF16)
    return hi, lo


def _normed_with_halo(xh_ref, x_ref, g, xn_ref):
    first = pl.program_id(1) == 0
    hn = _rms(xh_ref[0], g)
    xn_ref[0:HALO, :] = jnp.where(first, 0.0, hn).astype(xn_ref.dtype)
    xn_ref[HALO:, :] = _rms(x_ref[0], g).astype(xn_ref.dtype)


def _tile_specs(ts, d):
    per = ts // HALO
    halo = pl.BlockSpec((1, HALO, d), lambda b, s: (b, jnp.maximum(s * per - 1, 0), 0))
    tile = pl.BlockSpec((1, ts, d), lambda b, s: (b, s, 0))
    return halo, tile


def _const_spec(shape):
    zeros = (0,) * len(shape)
    return pl.BlockSpec(shape, lambda b, s: zeros, pipeline_mode=pl.Buffered(1))


def _ffn_kernel(xh_ref, x_ref, g_ref, win_ref, cw_ref, cb_ref, wout_ref, ng_ref, o_ref, *rest, ts, hidden):
    xn_ref, a_ref = rest[-2:]
    _normed_with_halo(xh_ref, x_ref, g_ref[...], xn_ref)
    assert FFN_CONV == 3
    for c in range(hidden // FFN_CHUNK):
        conv = []
        for part in range(2):
            cols = slice(part * hidden + c * FFN_CHUNK, part * hidden + (c + 1) * FFN_CHUNK)
            h = _dot(xn_ref[...], win_ref[:, cols])
            inner = cw_ref[1:2, cols] * h + pltpu.roll(cw_ref[0:1, cols] * h, 1, 0)
            y = (cw_ref[2:3, cols] * h + cb_ref[:, cols]) + pltpu.roll(inner, 1, 0)
            conv.append(y[HALO:, :])
        val, gate = conv
        half = 0.5 * gate
        a_ref[:, c * FFN_CHUNK:(c + 1) * FFN_CHUNK] = ((half + half * jnp.tanh(half)) * val).astype(BF16)
    out = x_ref[0] + _dot(a_ref[...], wout_ref[...])
    o_ref[0] = out
    if len(rest) == 3:
        rest[0][0] = _rms(out, ng_ref[...])


def _conv_ffn(x, g, w_in, conv_w, conv_b, w_out, *, ts, next_gain=None):
    bsz, seq, d = x.shape
    hidden = w_out.shape[0]
    halo, tile = _tile_specs(ts, d)
    emit_normed = next_gain is not None
    out = jax.ShapeDtypeStruct(x.shape, x.dtype)
    return pl.pallas_call(
        functools.partial(_ffn_kernel, ts=ts, hidden=hidden),
        grid=(bsz, seq // ts),
        in_specs=[halo, tile, _const_spec((1, d)), _const_spec((d, 2 * hidden)),
                  _const_spec((FFN_CONV, 2 * hidden)), _const_spec((1, 2 * hidden)),
                  _const_spec((hidden, d)), _const_spec((1, d))],
        out_specs=[tile, tile] if emit_normed else tile,
        out_shape=[out, out] if emit_normed else out,
        scratch_shapes=[pltpu.VMEM((HALO + ts, d), BF16),
                        pltpu.VMEM((ts, hidden), BF16)],
        compiler_params=_cparams(),
    )(x, x, g.reshape(1, d), w_in.astype(BF16), conv_w, conv_b.reshape(1, -1), w_out.astype(BF16),
      (next_gain if emit_normed else g).reshape(1, d))


def _pool_kernel(xh_ref, x_ref, g_ref, w_ref, b_ref, sc_ref, o_ref, hn_ref, *, ts, group):
    _normed_with_halo(xh_ref, x_ref, g_ref[...], hn_ref)
    pos = pl.program_id(1) * ts + lax.broadcasted_iota(jnp.int32, (ts, 1), 0)
    for gi, win in enumerate(POOL_WINDOWS):
        lanes = slice(gi * group, (gi + 1) * group)
        acc = hn_ref[:, lanes]
        shift = 1
        while shift < win:
            acc = acc + pltpu.roll(acc, shift, 0)
            shift *= 2
        h = hn_ref[HALO:HALO + ts, lanes]
        count = jnp.minimum(pos + 1, win).astype(F32)
        diff = acc[HALO:, :] / count - h
        y = _dot(diff.astype(BF16), w_ref[gi]) + b_ref[:, lanes]
        o_ref[0, :, lanes] = x_ref[0, :, lanes] + sc_ref[:, lanes] * y


def _pool_layer(x, g, w, b, scale, *, ts):
    bsz, seq, d = x.shape
    ngroup, group = w.shape[0], w.shape[1]
    halo, tile = _tile_specs(ts, d)
    return pl.pallas_call(
        functools.partial(_pool_kernel, ts=ts, group=group),
        grid=(bsz, seq // ts),
        in_specs=[halo, tile, _const_spec((1, d)), _const_spec((ngroup, group, group)),
                  _const_spec((1, d)), _const_spec((1, d))],
        out_specs=tile,
        out_shape=jax.ShapeDtypeStruct(x.shape, x.dtype),
        scratch_shapes=[pltpu.VMEM((HALO + ts, d), F32)],
        compiler_params=_cparams(),
    )(x, x, g.reshape(1, d), w.astype(BF16), b.reshape(1, d), scale.reshape(1, d))


S5_CHUNK = 8
S5_LANE_CHUNK = 256


S5_SEQS = 2


def _s5_core_kernel(u_ref, wz_ref, tz_ref, wexp_ref, pw_ref, dsk_ref, o_ref, xc_ref, z_ref, y_ref, *, seq):
    m = seq // S5_CHUNK
    sl = S5_SLAB
    seqs = range(S5_SEQS)
    for b in seqs:
        for r in range(S5_CHUNK):
            xc_ref[b * m:(b + 1) * m, r * sl:(r + 1) * sl] = (
                u_ref[b, pl.ds(r, m, stride=S5_CHUNK), :].astype(BF16))
    half = z_ref.shape[2] // 2
    own = _dot(xc_ref[...], wz_ref[0])
    for b in seqs:
        z_ref[b, 0:8, :] = jnp.zeros((8, 2 * half), F32)
        z_ref[b, 8:, :] = own[b * m:(b + 1) * m]

    lc = S5_LANE_CHUNK
    for c in range(half // lc):
        re = slice(c * lc, (c + 1) * lc)
        im = slice(half + c * lc, half + (c + 1) * lc)
        pws = [(pw_ref[0, i, 0, :, re], pw_ref[0, i, 1, :, re]) for i in range(4)]

        def body(i, carry, re=re, im=im, pws=pws):
            r0 = pl.multiple_of(8 + i * 8, 8)
            out = []
            for b in seqs:
                c_re, c_im = carry[b]
                b_re = z_ref[b, pl.ds(r0, 8), re]
                b_im = z_ref[b, pl.ds(r0, 8), im]
                for step, shift in enumerate((1, 2, 4)):
                    p_re, p_im = pws[step]
                    s_re = pltpu.roll(b_re, shift, 0)
                    s_im = pltpu.roll(b_im, shift, 0)
                    b_re, b_im = (b_re + (p_re * s_re - p_im * s_im),
                                  b_im + (p_re * s_im + p_im * s_re))
                p_re, p_im = pws[3]
                h_re = b_re + (p_re * c_re - p_im * c_im)
                h_im = b_im + (p_re * c_im + p_im * c_re)
                z_ref[b, pl.ds(r0, 8), re] = h_re
                z_ref[b, pl.ds(r0, 8), im] = h_im
                out.append((jnp.broadcast_to(h_re[7:8, :], (8, lc)), jnp.broadcast_to(h_im[7:8, :], (8, lc))))
            return tuple(out)

        zero = jnp.zeros((8, lc), F32)
        lax.fori_loop(0, m // 8, body, tuple((zero, zero) for _ in seqs))

    before = jnp.concatenate([z_ref[b, 7:7 + m, :].astype(BF16) for b in seqs], axis=0)
    yc = _dot(xc_ref[...], tz_ref[0]) + _dot(before, wexp_ref[0])
    for b in seqs:
        for r in range(S5_CHUNK):
            rows = pl.ds(r, m, stride=S5_CHUNK)
            y_ref[b, rows, :] = _gelu(yc[b * m:(b + 1) * m, r * sl:(r + 1) * sl] + dsk_ref[...] * u_ref[b, rows, :])
    o_ref[...] = y_ref[...].astype(BF16)


def _s5_out_kernel(x_ref, y_ref, w_ref, b_ref, o_ref, *, d):
    z = _dot(y_ref[0], w_ref[...]) + b_ref[...]
    o_ref[0] = x_ref[0] + z[:, :d] * _sigmoid(z[:, d:])


def _s5_expand_kernel(a_ref, e_ref, o_ref, *, row_div, col_div, per):
    rows, cols = o_ref.shape[1:]
    full = _dot(a_ref[0], e_ref[...])
    row_group = (lax.broadcasted_iota(jnp.int32, (rows, cols), 0) >> (row_div.bit_length() - 1)) & (per - 1)
    col_group = (lax.broadcasted_iota(jnp.int32, (rows, cols), 1) >> (col_div.bit_length() - 1)) & (per - 1)
    o_ref[0] = jnp.where(row_group == col_group, full, 0.0).astype(BF16)


def _s5_expand(compact, src_of_col, *, row_div, col_div, per):
    nslab, rows, k = compact.shape
    cols = src_of_col.shape[0]
    select = (jnp.arange(k)[:, None] == src_of_col[None, :]).astype(BF16)
    return pl.pallas_call(
        functools.partial(_s5_expand_kernel, row_div=row_div, col_div=col_div, per=per),
        grid=(nslab,),
        in_specs=[pl.BlockSpec((1, rows, k), lambda j: (j, 0, 0)), pl.BlockSpec((k, cols), lambda j: (0, 0))],
        out_specs=pl.BlockSpec((1, rows, cols), lambda j: (j, 0, 0)),
        out_shape=jax.ShapeDtypeStruct((nslab, rows, cols), BF16),
        compiler_params=pltpu.CompilerParams(dimension_semantics=("arbitrary",), vmem_limit_bytes=VMEM_LIMIT),
    )(compact.astype(BF16), select)


def _s5_tables(lam_re, lam_im, log_dt, b_re, b_im, c_re, c_im, d):
    ngroups, nstate = lam_re.shape
    per = S5_SLAB // S5_GROUP
    nslab = d // S5_SLAB
    nl = S5_CHUNK
    hh = S5_GROUP
    lam = lax.complex(jnp.minimum(lam_re.astype(F32), -1e-4), lam_im.astype(F32))
    dt = jnp.exp(log_dt.astype(F32))[:, None]
    lam_bar = jnp.exp(lam * dt)
    b_bar = ((lam_bar - 1.0) / lam)[..., None] * lax.complex(b_re.astype(F32), b_im.astype(F32))
    cc = lax.complex(c_re.astype(F32), c_im.astype(F32))
    powers = [jnp.ones_like(lam_bar)]
    for _ in range(nl):
        powers.append(powers[-1] * lam_bar)
    powers = jnp.stack(powers)
    rows = nl * S5_SLAB
    col = jnp.arange(rows)

    t = powers[nl - 1::-1][:nl, :, :, None] * b_bar[None]
    t = t.reshape(nl, nslab, per, nstate, hh).transpose(1, 0, 2, 4, 3)
    wz = _s5_expand(jnp.stack([jnp.real(t), jnp.imag(t)], axis=4).reshape(nslab, rows, 2 * nstate),
                    (col // (per * nstate)) * nstate + col % nstate, row_div=hh, col_div=nstate, per=per)

    kern = jnp.real(jnp.einsum('gip,kgp,gpj->gkij', cc, powers[:nl], b_bar))
    lag = jnp.arange(nl)[None, :] - jnp.arange(nl)[:, None]
    kl = jnp.where((lag >= 0)[None, :, :, None, None], kern[:, jnp.clip(lag, 0, nl - 1)], 0.0)
    kl = kl.reshape(nslab, per, nl, nl, hh, hh).transpose(0, 2, 1, 5, 3, 4)
    step_of_col = (col // S5_SLAB) * hh + col % hh
    tz = _s5_expand(kl.reshape(nslab, rows, nl * hh), step_of_col, row_div=hh, col_div=hh, per=per)

    e = cc[None] * powers[1:nl + 1, :, None, :]
    e = e.reshape(nl, nslab, per, hh, nstate).transpose(1, 2, 4, 0, 3)
    wexp = _s5_expand(jnp.stack([jnp.real(e), -jnp.imag(e)], axis=1).reshape(nslab, 2 * per * nstate, nl * hh),
                      step_of_col, row_div=nstate, col_div=hh, per=per)

    step1 = powers[nl].reshape(nslab, 1, per * nstate)
    step2 = step1 * step1
    step4 = step2 * step2
    row = jnp.arange(8)[None, :, None]
    tabs = [jnp.where(row >= s, p, 0.0) for s, p in ((1, step1), (2, step2), (4, step4))]
    ramp = [step1]
    for _ in range(7):
        ramp.append(ramp[-1] * step1)
    tabs.append(jnp.concatenate(ramp, axis=1))
    pw = jnp.stack(tabs, axis=1)
    pw = jnp.stack([jnp.real(pw), jnp.imag(pw)], axis=2).astype(F32)
    return wz, tz, wexp, pw


def _s5_layer(x, u, lam_re, lam_im, log_dt, b_re, b_im, c_re, c_im, d_skip, w_out, b_out, *, ts):
    bsz, seq, d = x.shape
    assert seq % (8 * S5_CHUNK) == 0 and bsz % S5_SEQS == 0, (bsz, seq)
    wz, tz, wexp, pw = _s5_tables(lam_re, lam_im, log_dt, b_re, b_im, c_re, c_im, d)
    nslab = d // S5_SLAB
    tile = pl.BlockSpec((1, ts, d), lambda b, s: (b, s, 0))
    slab = pl.BlockSpec((S5_SEQS, seq, S5_SLAB), lambda j, b: (b, 0, j))
    m = seq // S5_CHUNK
    cols = pw.shape[-1]

    def per_slab(shape):
        zeros = (0,) * (len(shape) - 1)
        return pl.BlockSpec((1,) + shape[1:], lambda j, b: (j,) + zeros, pipeline_mode=pl.Buffered(1))

    y = pl.pallas_call(
        functools.partial(_s5_core_kernel, seq=seq),
        grid=(nslab, bsz // S5_SEQS),
        in_specs=[slab, per_slab(wz.shape), per_slab(tz.shape), per_slab(wexp.shape), per_slab(pw.shape),
                  pl.BlockSpec((1, S5_SLAB), lambda j, b: (0, j))],
        out_specs=slab,
        out_shape=jax.ShapeDtypeStruct(x.shape, BF16),
        scratch_shapes=[pltpu.VMEM((S5_SEQS * m, S5_CHUNK * S5_SLAB), BF16),
                        pltpu.VMEM((S5_SEQS, 8 + m, 2 * cols), F32),
                        pltpu.VMEM((S5_SEQS, seq, S5_SLAB), F32)],
        compiler_params=_cparams(),
    )(u, wz, tz, wexp, pw, d_skip.reshape(1, d))

    return pl.pallas_call(
        functools.partial(_s5_out_kernel, d=d),
        grid=(bsz, seq // ts),
        in_specs=[tile, tile, _const_spec((d, 2 * d)), _const_spec((1, 2 * d))],
        out_specs=tile,
        out_shape=jax.ShapeDtypeStruct(x.shape, x.dtype),
        compiler_params=_cparams(),
    )(x, y, w_out.astype(BF16), b_out.reshape(1, 2 * d))


def _lru_kernel(xh_ref, x_ref, g_ref, win_ref, cw_ref, cb_ref, wa_ref, ba_ref, wx_ref, bx_ref, lam_ref,
                wout_ref, o_ref, xn_ref, gb_ref, rec_ref, a_ref, b_ref, carry_ref, *, ts, width):
    _normed_with_halo(xh_ref, x_ref, g_ref[...], xn_ref)
    gb_ref[...] = _dot(xn_ref[HALO:, :], win_ref[:, :width])
    rec_ref[...] = _dot(xn_ref[...], win_ref[:, width:])
    block = width // LRU_BLOCKS
    for n in range(LRU_BLOCKS):
        lanes = slice(n * block, (n + 1) * block)
        h = rec_ref[:, lanes]
        acc = cw_ref[0:1, lanes] * h
        for k in range(1, LRU_CONV):
            acc = cw_ref[k:k + 1, lanes] * h + pltpu.roll(acc, 1, 0)
        rec = acc[HALO:, :] + cb_ref[:, lanes]
        rb = rec.astype(BF16)
        r = _sigmoid(_dot(rb, wa_ref[n]) + ba_ref[:, lanes])
        i = _sigmoid(_dot(rb, wx_ref[n]) + bx_ref[:, lanes])
        log_a = (-LRU_C * r) * _softplus(-lam_ref[:, lanes])
        a_ref[:, lanes] = jnp.exp(log_a)
        th = jnp.tanh(log_a)
        b_ref[:, lanes] = jnp.sqrt(-2.0 * th / (1.0 - th)) * (i * rec)

    @pl.when(pl.program_id(1) == 0)
    def _():
        carry_ref[...] = jnp.zeros_like(carry_ref)

    row = lax.broadcasted_iota(jnp.int32, (8, width), 0)

    def body(i, carry):
        r0 = pl.multiple_of(i * 8, 8)
        a = a_ref[pl.ds(r0, 8), :]
        b = b_ref[pl.ds(r0, 8), :]
        for shift in (1, 2, 4):
            has_partner = row >= shift
            a_prev = jnp.where(has_partner, pltpu.roll(a, shift, 0), 1.0)
            b_prev = jnp.where(has_partner, pltpu.roll(b, shift, 0), 0.0)
            b = a * b_prev + b
            a = a * a_prev
        h = a * carry + b
        b_ref[pl.ds(r0, 8), :] = h
        return jnp.broadcast_to(h[7:8, :], (8, width))

    carry_ref[...] = lax.fori_loop(0, ts // 8, body, carry_ref[...], unroll=2)
    y = (_gelu(gb_ref[...]) * b_ref[...]).astype(BF16)
    o_ref[0] = x_ref[0] + _dot(y, wout_ref[...])


def _lru_layer(x, g, w_in, conv_w, conv_b, w_a, b_a, w_x, b_x, lam, w_out, *, ts):
    bsz, seq, d = x.shape
    width = w_out.shape[0]
    halo, tile = _tile_specs(ts, d)
    row = lambda v: v.reshape(1, -1)
    return pl.pallas_call(
        functools.partial(_lru_kernel, ts=ts, width=width),
        grid=(bsz, seq // ts),
        in_specs=[halo, tile, _const_spec((1, d)), _const_spec((d, 2 * width)),
                  _const_spec((LRU_CONV, width)), _const_spec((1, width)),
                  _const_spec(w_a.shape), _const_spec((1, width)),
                  _const_spec(w_x.shape), _const_spec((1, width)), _const_spec((1, width)),
                  _const_spec((width, d))],
        out_specs=tile,
        out_shape=jax.ShapeDtypeStruct(x.shape, x.dtype),
        scratch_shapes=[pltpu.VMEM((HALO + ts, d), BF16),
                        pltpu.VMEM((ts, width), F32),
                        pltpu.VMEM((HALO + ts, width), F32),
                        pltpu.VMEM((ts, width), F32), pltpu.VMEM((ts, width), F32),
                        pltpu.VMEM((8, width), F32)],
        compiler_params=_cparams(),
    )(x, x, row(g), w_in.astype(BF16), conv_w, row(conv_b), w_a.astype(BF16), row(b_a),
      w_x.astype(BF16), row(b_x), row(lam), w_out.astype(BF16))


def _qkv_kernel(x_ref, g_ref, w_ref, qg_ref, kg_ref, gsum_ref, gexp_ref, q_ref, k_ref, v_ref, *, d):
    xn = _rms(x_ref[0], g_ref[...]).astype(BF16)

    def head_normed(t, gain):
        hi, lo = _split(t * t)
        ms = _dot(hi, gsum_ref[...]) + _dot(lo, gsum_ref[...])
        hi, lo = _split(lax.rsqrt(ms + EPS))
        return t * (_dot(hi, gexp_ref[...]) + _dot(lo, gexp_ref[...])) * gain

    scale = LOG2E / math.sqrt(SB_HEAD_DIM)
    q_ref[0] = (head_normed(_dot(xn, w_ref[:, :d]), qg_ref[...]) * scale).astype(BF16)
    k_ref[0] = head_normed(_dot(xn, w_ref[:, d:2 * d]), kg_ref[...]).astype(BF16)
    v_ref[0] = _dot(xn, w_ref[:, 2 * d:]).astype(BF16)


def _qkv_proj(x, g, w_qkv, q_g, k_g, *, ts):
    bsz, seq, d = x.shape
    heads = d // SB_HEAD_DIM
    head_of_lane = jnp.arange(d) // SB_HEAD_DIM
    onehot = (head_of_lane[:, None] == jnp.arange(128)[None, :])
    gsum = (onehot.astype(F32) / SB_HEAD_DIM).astype(BF16)
    gexp = onehot.T.astype(BF16)
    tile = pl.BlockSpec((1, ts, d), lambda b, s: (b, s, 0))
    out = jax.ShapeDtypeStruct(x.shape, BF16)
    return pl.pallas_call(
        functools.partial(_qkv_kernel, d=d),
        grid=(bsz, seq // ts),
        in_specs=[tile, _const_spec((1, d)), _const_spec((d, 3 * d)), _const_spec((1, d)),
                  _const_spec((1, d)), _const_spec((d, 128)), _const_spec((128, d))],
        out_specs=[tile, tile, tile],
        out_shape=[out, out, out],
        compiler_params=_cparams(),
    )(x, g.reshape(1, d), w_qkv.astype(BF16), jnp.tile(q_g, heads).reshape(1, d),
      jnp.tile(k_g, heads).reshape(1, d), gsum, gexp)


SB_TILE = 256
SB_SUB = 128
SB_PAIRS = 4


def _attn_kernel(q_ref, k_ref, v_ref, tri_ref, o_ref, acc_ref, run_ref):
    tq, sub, pair = SB_TILE, SB_SUB, 2 * SB_HEAD_DIM
    qi = pl.program_id(2)
    lane = lax.broadcasted_iota(jnp.int32, (tq, pair), 1)
    first_head = lane < SB_HEAD_DIM
    t_pos = lax.broadcasted_iota(jnp.int32, (tq, tq), 0)
    s_pos = lax.broadcasted_iota(jnp.int32, (tq, tq), 1)
    causal = s_pos < t_pos
    q_heads = []
    for p in range(SB_PAIRS):
        q = q_ref[0, :, p * pair:(p + 1) * pair]
        zero = jnp.zeros_like(q)
        q_heads += [jnp.where(first_head, q, zero), jnp.where(first_head, zero, q)]

    def block(kb, diagonal):
        k0 = pl.multiple_of(kb * tq, tq)
        heads = range(2 * SB_PAIRS)
        kblk = [k_ref[0, pl.ds(k0, tq), p * pair:(p + 1) * pair] for p in range(SB_PAIRS)]
        vblk = [v_ref[0, pl.ds(k0, tq), p * pair:(p + 1) * pair] for p in range(SB_PAIRS)]
        z = [lax.dot_general(q_heads[h], kblk[h // 2], (((1,), (1,)), ((), ())), preferred_element_type=F32)
             for h in heads]
        halves = []
        for h in heads:
            neg_abs = lax.bitcast_convert_type(
                lax.bitcast_convert_type(z[h], jnp.uint32) | jnp.uint32(0x80000000), F32)
            sp = jnp.maximum(z[h], 0.0) + jnp.log(1.0 + jnp.exp2(neg_abs)) * LOG2E
            if diagonal:
                sp = jnp.where(causal, sp, 0.0)
            halves.append(_split(sp))
        sums = [[_dot(jnp.concatenate([hi[:, c * sub:(c + 1) * sub], lo[:, c * sub:(c + 1) * sub]], axis=1),
                      tri_ref[...]) for c in range(tq // sub)] for hi, lo in halves]
        atts = []
        for h in heads:
            run = run_ref[h]
            parts = []
            for c in reversed(range(tq // sub)):
                both = sums[h][c] + jnp.concatenate([run, run], axis=1)
                parts.append(jnp.exp2(z[h][:, c * sub:(c + 1) * sub] + both[:, :sub]))
                run = both[:, sub:]
            run_ref[h] = run
            att = jnp.concatenate(parts[::-1], axis=1)
            if diagonal:
                att = jnp.where(causal, att, 0.0)
            atts.append(att.astype(BF16))
        for h in heads:
            acc_ref[h] += _dot(atts[h], vblk[h // 2])

    acc_ref[...] = jnp.zeros_like(acc_ref)
    run_ref[...] = jnp.zeros_like(run_ref)
    block(qi, True)

    def earlier(j, carry):
        block(qi - 1 - j, False)
        return carry

    lax.fori_loop(0, qi, earlier, 0)
    for p in range(SB_PAIRS):
        o_ref[0, :, p * pair:(p + 1) * pair] = jnp.where(
            first_head, acc_ref[2 * p], acc_ref[2 * p + 1]).astype(BF16)


def _attention(q, k, v):
    bsz, seq, d = q.shape
    tq, sub = SB_TILE, SB_SUB
    lanes = SB_PAIRS * 2 * SB_HEAD_DIM
    j = jnp.arange(2 * sub)[:, None] % sub
    s = jnp.arange(2 * sub)[None, :]
    tri = -((s >= sub) | (j >= s)).astype(BF16)
    qspec = pl.BlockSpec((1, tq, lanes), lambda b, h, i: (b, i, h))
    kvspec = pl.BlockSpec((1, seq, lanes), lambda b, h, i: (b, 0, h))
    return pl.pallas_call(
        _attn_kernel,
        grid=(bsz, d // lanes, seq // tq),
        in_specs=[qspec, kvspec, kvspec, pl.BlockSpec((2 * sub, 2 * sub), lambda b, h, i: (0, 0))],
        out_specs=qspec,
        out_shape=jax.ShapeDtypeStruct(q.shape, BF16),
        scratch_shapes=[pltpu.VMEM((2 * SB_PAIRS, tq, 2 * SB_HEAD_DIM), F32),
                        pltpu.VMEM((2 * SB_PAIRS, tq, sub), F32)],
        compiler_params=pltpu.CompilerParams(
            dimension_semantics=("arbitrary", "arbitrary", "arbitrary"), vmem_limit_bytes=VMEM_LIMIT),
    )(q, k, v, tri)


def _proj_kernel(x_ref, a_ref, w_ref, o_ref):
    o_ref[0] = x_ref[0] + _dot(a_ref[0], w_ref[...])


def _residual_proj(x, a, w, *, ts):
    bsz, seq, d = x.shape
    tile = pl.BlockSpec((1, ts, d), lambda b, s: (b, s, 0))
    atile = pl.BlockSpec((1, ts, a.shape[-1]), lambda b, s: (b, s, 0))
    return pl.pallas_call(
        _proj_kernel,
        grid=(bsz, seq // ts),
        in_specs=[tile, atile, _const_spec(w.shape)],
        out_specs=tile,
        out_shape=jax.ShapeDtypeStruct(x.shape, x.dtype),
        compiler_params=_cparams(),
    )(x, a, w.astype(BF16))


def _sb_layer(x, g, w_qkv, q_g, k_g, w_o, *, ts):
    q, k, v = _qkv_proj(x, g, w_qkv, q_g, k_g, ts=ts)
    return _residual_proj(x, _attention(q, k, v), w_o, ts=ts)


def _tile_rows(seq, want):
    ts = min(seq, want)
    assert seq % ts == 0 and ts % HALO == 0, (seq, ts)
    return ts


def kernel(x, norm_mix_g, norm_ffn_g, pool_w, pool_b, pool_scale, s5_lam_re, s5_lam_im, s5_log_dt, s5_b_re, s5_b_im, s5_c_re, s5_c_im, s5_d, s5_w_out, s5_b_out, lru_w_in, lru_conv_w, lru_conv_b, lru_w_a, lru_b_a, lru_w_x, lru_b_x, lru_lam, lru_w_out, sb_w_qkv, sb_q_g, sb_k_g, sb_w_o, ffn_w_in, ffn_conv_w, ffn_conv_b, ffn_w_out):
    depth = norm_mix_g.shape[0]
    seq = x.shape[1]
    normed = None
    for layer in range(depth):
        m, j = layer % 4, layer // 4
        g = norm_mix_g[layer]
        if m == 0:
            x = _pool_layer(x, g, pool_w[j], pool_b[j], pool_scale[j], ts=_tile_rows(seq, 512))
        elif m == 1:
            x = _s5_layer(x, normed, s5_lam_re[j], s5_lam_im[j], s5_log_dt[j], s5_b_re[j], s5_b_im[j],
                          s5_c_re[j], s5_c_im[j], s5_d[j], s5_w_out[j], s5_b_out[j],
                          ts=_tile_rows(seq, 512))
        elif m == 2:
            x = _lru_layer(x, g, lru_w_in[j], lru_conv_w[j], lru_conv_b[j], lru_w_a[j], lru_b_a[j],
                           lru_w_x[j], lru_b_x[j], lru_lam[j], lru_w_out[j], ts=_tile_rows(seq, 512))
        else:
            x = _sb_layer(x, g, sb_w_qkv[j], sb_q_g[j], sb_k_g[j], sb_w_o[j], ts=_tile_rows(seq, 512))
        s5_next = layer + 1 < depth and (layer + 1) % 4 == 1
        out = _conv_ffn(x, norm_ffn_g[layer], ffn_w_in[layer], ffn_conv_w[layer], ffn_conv_b[layer],
                        ffn_w_out[layer], ts=_tile_rows(seq, 1024),
                        next_gain=norm_mix_g[layer + 1] if s5_next else None)
        x, normed = out if s5_next else (out, None)
    return x
```

```python
import functools
import math

import jax
import jax.numpy as jnp
from jax import lax
from jax.experimental import pallas as pl
from jax.experimental.pallas import tpu as pltpu

F32 = jnp.float32
BF16 = jnp.bfloat16

EPS = 1e-6
LOG2E = 1.0 / math.log(2.0)
HALO = 16
POOL_WINDOWS = (2, 4, 8, 16)
S5_GROUP = 16
S5_STATE = 64
S5_SLAB = 128
LRU_BLOCKS = 4
LRU_CONV = 4
LRU_C = 8.0
SB_HEAD_DIM = 64
FFN_CONV = 3
FFN_CHUNK = 256
VMEM_LIMIT = 56 * 1024 * 1024


def _cparams():
    return pltpu.CompilerParams(
        dimension_semantics=("arbitrary", "arbitrary"),
        vmem_limit_bytes=VMEM_LIMIT)


def _rms(x, g):
    ms = jnp.mean(x * x, axis=-1, keepdims=True)
    return x * lax.rsqrt(ms + EPS) * g


def _gelu(x):
    c = math.sqrt(2.0 / math.pi)
    return 0.5 * x * (1.0 + jnp.tanh(c * (x + 0.044715 * (x * x * x))))


def _sigmoid(x):
    return 1.0 / (1.0 + jnp.exp(-x))


def _softplus(x):
    return jnp.maximum(x, 0.0) + jnp.log(1.0 + jnp.exp(-jnp.abs(x)))


def _dot(a, b):
    return jnp.dot(a, b, preferred_element_type=F32)


def _split(x):
    hi = x.astype(BF16)
    lo = (x - hi.astype(F32)).astype(BF16)
    return hi, lo


def _normed_with_halo(xh_ref, x_ref, g, xn_ref):
    first = pl.program_id(1) == 0
    hn = _rms(xh_ref[0], g)
    xn_ref[0:HALO, :] = jnp.where(first, 0.0, hn).astype(xn_ref.dtype)
    xn_ref[HALO:, :] = _rms(x_ref[0], g).astype(xn_ref.dtype)


def _tile_specs(ts, d):
    per = ts // HALO
    halo = pl.BlockSpec((1, HALO, d), lambda b, s: (b, jnp.maximum(s * per - 1, 0), 0))
    tile = pl.BlockSpec((1, ts, d), lambda b, s: (b, s, 0))
    return halo, tile


def _const_spec(shape):
    zeros = (0,) * len(shape)
    return pl.BlockSpec(shape, lambda b, s: zeros, pipeline_mode=pl.Buffered(1))


def _ffn_kernel(xh_ref, x_ref, g_ref, win_ref, cw_ref, cb_ref, wout_ref, ng_ref, o_ref, *rest, ts, hidden):
    xn_ref, a_ref = rest[-2:]
    _normed_with_halo(xh_ref, x_ref, g_ref[...], xn_ref)
    assert FFN_CONV == 3
    for c in range(hidden // FFN_CHUNK):
        conv = []
        for part in range(2):
            cols = slice(part * hidden + c * FFN_CHUNK, part * hidden + (c + 1) * FFN_CHUNK)
            h = _dot(xn_ref[...], win_ref[:, cols])
            inner = cw_ref[1:2, cols] * h + pltpu.roll(cw_ref[0:1, cols] * h, 1, 0)
            y = (cw_ref[2:3, cols] * h + cb_ref[:, cols]) + pltpu.roll(inner, 1, 0)
            conv.append(y[HALO:, :])
        val, gate = conv
        half = 0.5 * gate
        a_ref[:, c * FFN_CHUNK:(c + 1) * FFN_CHUNK] = ((half + half * jnp.tanh(half)) * val).astype(BF16)
    out = x_ref[0] + _dot(a_ref[...], wout_ref[...])
    o_ref[0] = out
    if len(rest) == 3:
        rest[0][0] = _rms(out, ng_ref[...])


def _conv_ffn(x, g, w_in, conv_w, conv_b, w_out, *, ts, next_gain=None):
    bsz, seq, d = x.shape
    hidden = w_out.shape[0]
    halo, tile = _tile_specs(ts, d)
    emit_normed = next_gain is not None
    out = jax.ShapeDtypeStruct(x.shape, x.dtype)
    return pl.pallas_call(
        functools.partial(_ffn_kernel, ts=ts, hidden=hidden),
        grid=(bsz, seq // ts),
        in_specs=[halo, tile, _const_spec((1, d)), _const_spec((d, 2 * hidden)),
                  _const_spec((FFN_CONV, 2 * hidden)), _const_spec((1, 2 * hidden)),
                  _const_spec((hidden, d)), _const_spec((1, d))],
        out_specs=[tile, tile] if emit_normed else tile,
        out_shape=[out, out] if emit_normed else out,
        scratch_shapes=[pltpu.VMEM((HALO + ts, d), BF16),
                        pltpu.VMEM((ts, hidden), BF16)],
        compiler_params=_cparams(),
    )(x, x, g.reshape(1, d), w_in.astype(BF16), conv_w, conv_b.reshape(1, -1), w_out.astype(BF16),
      (next_gain if emit_normed else g).reshape(1, d))


def _pool_kernel(xh_ref, x_ref, g_ref, w_ref, b_ref, sc_ref, o_ref, hn_ref, *, ts, group):
    _normed_with_halo(xh_ref, x_ref, g_ref[...], hn_ref)
    pos = pl.program_id(1) * ts + lax.broadcasted_iota(jnp.int32, (ts, 1), 0)
    for gi, win in enumerate(POOL_WINDOWS):
        lanes = slice(gi * group, (gi + 1) * group)
        acc = hn_ref[:, lanes]
        shift = 1
        while shift < win:
            acc = acc + pltpu.roll(acc, shift, 0)
            shift *= 2
        h = hn_ref[HALO:HALO + ts, lanes]
        count = jnp.minimum(pos + 1, win).astype(F32)
        diff = acc[HALO:, :] / count - h
        y = _dot(diff.astype(BF16), w_ref[gi]) + b_ref[:, lanes]
        o_ref[0, :, lanes] = x_ref[0, :, lanes] + sc_ref[:, lanes] * y


def _pool_layer(x, g, w, b, scale, *, ts):
    bsz, seq, d = x.shape
    ngroup, group = w.shape[0], w.shape[1]
    halo, tile = _tile_specs(ts, d)
    return pl.pallas_call(
        functools.partial(_pool_kernel, ts=ts, group=group),
        grid=(bsz, seq // ts),
        in_specs=[halo, tile, _const_spec((1, d)), _const_spec((ngroup, group, group)),
                  _const_spec((1, d)), _const_spec((1, d))],
        out_specs=tile,
        out_shape=jax.ShapeDtypeStruct(x.shape, x.dtype),
        scratch_shapes=[pltpu.VMEM((HALO + ts, d), F32)],
        compiler_params=_cparams(),
    )(x, x, g.reshape(1, d), w.astype(BF16), b.reshape(1, d), scale.reshape(1, d))


S5_CHUNK = 8
S5_LANE_CHUNK = 256


S5_SEQS = 2


def _s5_core_kernel(u_ref, wz_ref, tz_ref, wexp_ref, pw_ref, dsk_ref, o_ref, xc_ref, z_ref, y_ref, *, seq):
    m = seq // S5_CHUNK
    sl = S5_SLAB
    seqs = range(S5_SEQS)
    for b in seqs:
        for r in range(S5_CHUNK):
            xc_ref[b * m:(b + 1) * m, r * sl:(r + 1) * sl] = (
                u_ref[b, pl.ds(r, m, stride=S5_CHUNK), :].astype(BF16))
    half = z_ref.shape[2] // 2
    own = _dot(xc_ref[...], wz_ref[0])
    for b in seqs:
        z_ref[b, 0:8, :] = jnp.zeros((8, 2 * half), F32)
        z_ref[b, 8:, :] = own[b * m:(b + 1) * m]

    lc = S5_LANE_CHUNK
    for c in range(half // lc):
        re = slice(c * lc, (c + 1) * lc)
        im = slice(half + c * lc, half + (c + 1) * lc)
        pws = [(pw_ref[0, i, 0, :, re], pw_ref[0, i, 1, :, re]) for i in range(4)]

        def body(i, carry, re=re, im=im, pws=pws):
            r0 = pl.multiple_of(8 + i * 8, 8)
            out = []
            for b in seqs:
                c_re, c_im = carry[b]
                b_re = z_ref[b, pl.ds(r0, 8), re]
                b_im = z_ref[b, pl.ds(r0, 8), im]
                for step, shift in enumerate((1, 2, 4)):
                    p_re, p_im = pws[step]
                    s_re = pltpu.roll(b_re, shift, 0)
                    s_im = pltpu.roll(b_im, shift, 0)
                    b_re, b_im = (b_re + (p_re * s_re - p_im * s_im),
                                  b_im + (p_re * s_im + p_im * s_re))
                p_re, p_im = pws[3]
                h_re = b_re + (p_re * c_re - p_im * c_im)
                h_im = b_im + (p_re * c_im + p_im * c_re)
                z_ref[b, pl.ds(r0, 8), re] = h_re
                z_ref[b, pl.ds(r0, 8), im] = h_im
                out.append((jnp.broadcast_to(h_re[7:8, :], (8, lc)), jnp.broadcast_to(h_im[7:8, :], (8, lc))))
            return tuple(out)

        zero = jnp.zeros((8, lc), F32)
        lax.fori_loop(0, m // 8, body, tuple((zero, zero) for _ in seqs))

    before = jnp.concatenate([z_ref[b, 7:7 + m, :].astype(BF16) for b in seqs], axis=0)
    yc = _dot(xc_ref[...], tz_ref[0]) + _dot(before, wexp_ref[0])
    for b in seqs:
        for r in range(S5_CHUNK):
            rows = pl.ds(r, m, stride=S5_CHUNK)
            y_ref[b, rows, :] = _gelu(yc[b * m:(b + 1) * m, r * sl:(r + 1) * sl] + dsk_ref[...] * u_ref[b, rows, :])
    o_ref[...] = y_ref[...].astype(BF16)


def _s5_out_kernel(x_ref, y_ref, w_ref, b_ref, o_ref, *, d):
    z = _dot(y_ref[0], w_ref[...]) + b_ref[...]
    o_ref[0] = x_ref[0] + z[:, :d] * _sigmoid(z[:, d:])


def _s5_expand_kernel(a_ref, e_ref, o_ref, *, row_div, col_div, per):
    rows, cols = o_ref.shape[1:]
    full = _dot(a_ref[0], e_ref[...])
    row_group = (lax.broadcasted_iota(jnp.int32, (rows, cols), 0) >> (row_div.bit_length() - 1)) & (per - 1)
    col_group = (lax.broadcasted_iota(jnp.int32, (rows, cols), 1) >> (col_div.bit_length() - 1)) & (per - 1)
    o_ref[0] = jnp.where(row_group == col_group, full, 0.0).astype(BF16)


def _s5_expand(compact, src_of_col, *, row_div, col_div, per):
    nslab, rows, k = compact.shape
    cols = src_of_col.shape[0]
    select = (jnp.arange(k)[:, None] == src_of_col[None, :]).astype(BF16)
    return pl.pallas_call(
        functools.partial(_s5_expand_kernel, row_div=row_div, col_div=col_div, per=per),
        grid=(nslab,),
        in_specs=[pl.BlockSpec((1, rows, k), lambda j: (j, 0, 0)), pl.BlockSpec((k, cols), lambda j: (0, 0))],
        out_specs=pl.BlockSpec((1, rows, cols), lambda j: (j, 0, 0)),
        out_shape=jax.ShapeDtypeStruct((nslab, rows, cols), BF16),
        compiler_params=pltpu.CompilerParams(dimension_semantics=("arbitrary",), vmem_limit_bytes=VMEM_LIMIT),
    )(compact.astype(BF16), select)


def _s5_tables(lam_re, lam_im, log_dt, b_re, b_im, c_re, c_im, d):
    ngroups, nstate = lam_re.shape
    per = S5_SLAB // S5_GROUP
    nslab = d // S5_SLAB
    nl = S5_CHUNK
    hh = S5_GROUP
    lam = lax.complex(jnp.minimum(lam_re.astype(F32), -1e-4), lam_im.astype(F32))
    dt = jnp.exp(log_dt.astype(F32))[:, None]
    lam_bar = jnp.exp(lam * dt)
    b_bar = ((lam_bar - 1.0) / lam)[..., None] * lax.complex(b_re.astype(F32), b_im.astype(F32))
    cc = lax.complex(c_re.astype(F32), c_im.astype(F32))
    powers = [jnp.ones_like(lam_bar)]
    for _ in range(nl):
        powers.append(powers[-1] * lam_bar)
    powers = jnp.stack(powers)
    rows = nl * S5_SLAB
    col = jnp.arange(rows)

    t = powers[nl - 1::-1][:nl, :, :, None] * b_bar[None]
    t = t.reshape(nl, nslab, per, nstate, hh).transpose(1, 0, 2, 4, 3)
    wz = _s5_expand(jnp.stack([jnp.real(t), jnp.imag(t)], axis=4).reshape(nslab, rows, 2 * nstate),
                    (col // (per * nstate)) * nstate + col % nstate, row_div=hh, col_div=nstate, per=per)

    kern = jnp.real(jnp.einsum('gip,kgp,gpj->gkij', cc, powers[:nl], b_bar))
    lag = jnp.arange(nl)[None, :] - jnp.arange(nl)[:, None]
    kl = jnp.where((lag >= 0)[None, :, :, None, None], kern[:, jnp.clip(lag, 0, nl - 1)], 0.0)
    kl = kl.reshape(nslab, per, nl, nl, hh, hh).transpose(0, 2, 1, 5, 3, 4)
    step_of_col = (col // S5_SLAB) * hh + col % hh
    tz = _s5_expand(kl.reshape(nslab, rows, nl * hh), step_of_col, row_div=hh, col_div=hh, per=per)

    e = cc[None] * powers[1:nl + 1, :, None, :]
    e = e.reshape(nl, nslab, per, hh, nstate).transpose(1, 2, 4, 0, 3)
    wexp = _s5_expand(jnp.stack([jnp.real(e), -jnp.imag(e)], axis=1).reshape(nslab, 2 * per * nstate, nl * hh),
                      step_of_col, row_div=nstate, col_div=hh, per=per)

    step1 = powers[nl].reshape(nslab, 1, per * nstate)
    step2 = step1 * step1
    step4 = step2 * step2
    row = jnp.arange(8)[None, :, None]
    tabs = [jnp.where(row >= s, p, 0.0) for s, p in ((1, step1), (2, step2), (4, step4))]
    ramp = [step1]
    for _ in range(7):
        ramp.append(ramp[-1] * step1)
    tabs.append(jnp.concatenate(ramp, axis=1))
    pw = jnp.stack(tabs, axis=1)
    pw = jnp.stack([jnp.real(pw), jnp.imag(pw)], axis=2).astype(F32)
    return wz, tz, wexp, pw


def _s5_layer(x, u, lam_re, lam_im, log_dt, b_re, b_im, c_re, c_im, d_skip, w_out, b_out, *, ts):
    bsz, seq, d = x.shape
    assert seq % (8 * S5_CHUNK) == 0 and bsz % S5_SEQS == 0, (bsz, seq)
    wz, tz, wexp, pw = _s5_tables(lam_re, lam_im, log_dt, b_re, b_im, c_re, c_im, d)
    nslab = d // S5_SLAB
    tile = pl.BlockSpec((1, ts, d), lambda b, s: (b, s, 0))
    slab = pl.BlockSpec((S5_SEQS, seq, S5_SLAB), lambda j, b: (b, 0, j))
    m = seq // S5_CHUNK
    cols = pw.shape[-1]

    def per_slab(shape):
        zeros = (0,) * (len(shape) - 1)
        return pl.BlockSpec((1,) + shape[1:], lambda j, b: (j,) + zeros, pipeline_mode=pl.Buffered(1))

    y = pl.pallas_call(
        functools.partial(_s5_core_kernel, seq=seq),
        grid=(nslab, bsz // S5_SEQS),
        in_specs=[slab, per_slab(wz.shape), per_slab(tz.shape), per_slab(wexp.shape), per_slab(pw.shape),
                  pl.BlockSpec((1, S5_SLAB), lambda j, b: (0, j))],
        out_specs=slab,
        out_shape=jax.ShapeDtypeStruct(x.shape, BF16),
        scratch_shapes=[pltpu.VMEM((S5_SEQS * m, S5_CHUNK * S5_SLAB), BF16),
                        pltpu.VMEM((S5_SEQS, 8 + m, 2 * cols), F32),
                        pltpu.VMEM((S5_SEQS, seq, S5_SLAB), F32)],
        compiler_params=_cparams(),
    )(u, wz, tz, wexp, pw, d_skip.reshape(1, d))

    return pl.pallas_call(
        functools.partial(_s5_out_kernel, d=d),
        grid=(bsz, seq // ts),
        in_specs=[tile, tile, _const_spec((d, 2 * d)), _const_spec((1, 2 * d))],
        out_specs=tile,
        out_shape=jax.ShapeDtypeStruct(x.shape, x.dtype),
        compiler_params=_cparams(),
    )(x, y, w_out.astype(BF16), b_out.reshape(1, 2 * d))


def _lru_kernel(xh_ref, x_ref, g_ref, win_ref, cw_ref, cb_ref, wa_ref, ba_ref, wx_ref, bx_ref, lam_ref,
                wout_ref, o_ref, xn_ref, gb_ref, rec_ref, a_ref, b_ref, carry_ref, *, ts, width):
    _normed_with_halo(xh_ref, x_ref, g_ref[...], xn_ref)
    gb_ref[...] = _dot(xn_ref[HALO:, :], win_ref[:, :width])
    rec_ref[...] = _dot(xn_ref[...], win_ref[:, width:])
    block = width // LRU_BLOCKS
    for n in range(LRU_BLOCKS):
        lanes = slice(n * block, (n + 1) * block)
        h = rec_ref[:, lanes]
        acc = cw_ref[0:1, lanes] * h
        for k in range(1, LRU_CONV):
            acc = cw_ref[k:k + 1, lanes] * h + pltpu.roll(acc, 1, 0)
        rec = acc[HALO:, :] + cb_ref[:, lanes]
        rb = rec.astype(BF16)
        r = _sigmoid(_dot(rb, wa_ref[n]) + ba_ref[:, lanes])
        i = _sigmoid(_dot(rb, wx_ref[n]) + bx_ref[:, lanes])
        log_a = (-LRU_C * r) * _softplus(-lam_ref[:, lanes])
        a_ref[:, lanes] = jnp.exp(log_a)
        th = jnp.tanh(log_a)
        b_ref[:, lanes] = jnp.sqrt(-2.0 * th / (1.0 - th)) * (i * rec)

    @pl.when(pl.program_id(1) == 0)
    def _():
        carry_ref[...] = jnp.zeros_like(carry_ref)

    row = lax.broadcasted_iota(jnp.int32, (8, width), 0)

    def body(i, carry):
        r0 = pl.multiple_of(i * 8, 8)
        a = a_ref[pl.ds(r0, 8), :]
        b = b_ref[pl.ds(r0, 8), :]
        for shift in (1, 2, 4):
            has_partner = row >= shift
            a_prev = jnp.where(has_partner, pltpu.roll(a, shift, 0), 1.0)
            b_prev = jnp.where(has_partner, pltpu.roll(b, shift, 0), 0.0)
            b = a * b_prev + b
            a = a * a_prev
        h = a * carry + b
        b_ref[pl.ds(r0, 8), :] = h
        return jnp.broadcast_to(h[7:8, :], (8, width))

    carry_ref[...] = lax.fori_loop(0, ts // 8, body, carry_ref[...], unroll=2)
    y = (_gelu(gb_ref[...]) * b_ref[...]).astype(BF16)
    o_ref[0] = x_ref[0] + _dot(y, wout_ref[...])


def _lru_layer(x, g, w_in, conv_w, conv_b, w_a, b_a, w_x, b_x, lam, w_out, *, ts):
    bsz, seq, d = x.shape
    width = w_out.shape[0]
    halo, tile = _tile_specs(ts, d)
    row = lambda v: v.reshape(1, -1)
    return pl.pallas_call(
        functools.partial(_lru_kernel, ts=ts, width=width),
        grid=(bsz, seq // ts),
        in_specs=[halo, tile, _const_spec((1, d)), _const_spec((d, 2 * width)),
                  _const_spec((LRU_CONV, width)), _const_spec((1, width)),
                  _const_spec(w_a.shape), _const_spec((1, width)),
                  _const_spec(w_x.shape), _const_spec((1, width)), _const_spec((1, width)),
                  _const_spec((width, d))],
        out_specs=tile,
        out_shape=jax.ShapeDtypeStruct(x.shape, x.dtype),
        scratch_shapes=[pltpu.VMEM((HALO + ts, d), BF16),
                        pltpu.VMEM((ts, width), F32),
                        pltpu.VMEM((HALO + ts, width), F32),
                        pltpu.VMEM((ts, width), F32), pltpu.VMEM((ts, width), F32),
                        pltpu.VMEM((8, width), F32)],
        compiler_params=_cparams(),
    )(x, x, row(g), w_in.astype(BF16), conv_w, row(conv_b), w_a.astype(BF16), row(b_a),
      w_x.astype(BF16), row(b_x), row(lam), w_out.astype(BF16))


def _qkv_kernel(x_ref, g_ref, w_ref, qg_ref, kg_ref, q_ref, k_ref, v_ref, *, d):
    xn = _rms(x_ref[0], g_ref[...]).astype(BF16)
    pair = 2 * SB_HEAD_DIM
    first_head = lax.broadcasted_iota(jnp.int32, (x_ref.shape[1], pair), 1) < SB_HEAD_DIM

    def head_normed(t, gain_ref, out_ref, scale):
        for p in range(d // pair):
            lanes = slice(p * pair, (p + 1) * pair)
            tp = t[:, lanes]
            sq = tp * tp
            first = jnp.sum(jnp.where(first_head, sq, 0.0), axis=-1, keepdims=True)
            second = jnp.sum(jnp.where(first_head, 0.0, sq), axis=-1, keepdims=True)
            inv_first = lax.rsqrt(first * (1.0 / SB_HEAD_DIM) + EPS)
            inv_second = lax.rsqrt(second * (1.0 / SB_HEAD_DIM) + EPS)
            inv = jnp.where(first_head, inv_first, inv_second)
            out_ref[0, :, lanes] = (tp * inv * (gain_ref[:, lanes] * scale)).astype(BF16)

    head_normed(_dot(xn, w_ref[:, :d]), qg_ref, q_ref, LOG2E / math.sqrt(SB_HEAD_DIM))
    head_normed(_dot(xn, w_ref[:, d:2 * d]), kg_ref, k_ref, 1.0)
    v_ref[0] = _dot(xn, w_ref[:, 2 * d:]).astype(BF16)


def _qkv_proj(x, g, w_qkv, q_g, k_g, *, ts):
    bsz, seq, d = x.shape
    heads = d // SB_HEAD_DIM
    tile = pl.BlockSpec((1, ts, d), lambda b, s: (b, s, 0))
    out = jax.ShapeDtypeStruct(x.shape, BF16)
    return pl.pallas_call(
        functools.partial(_qkv_kernel, d=d),
        grid=(bsz, seq // ts),
        in_specs=[tile, _const_spec((1, d)), _const_spec((d, 3 * d)), _const_spec((1, d)),
                  _const_spec((1, d))],
        out_specs=[tile, tile, tile],
        out_shape=[out, out, out],
        compiler_params=_cparams(),
    )(x, g.reshape(1, d), w_qkv.astype(BF16), jnp.tile(q_g, heads).reshape(1, d),
      jnp.tile(k_g, heads).reshape(1, d))


SB_TILE = 256
SB_SUB = 128
SB_PAIRS = 8


def _attn_kernel(x_ref, q_ref, k_ref, v_ref, tri_ref, wo_ref, o_ref, acc_ref, run_ref, cat_ref):
    tq, sub, pair = SB_TILE, SB_SUB, 2 * SB_HEAD_DIM
    qi = pl.program_id(1)
    lane = lax.broadcasted_iota(jnp.int32, (tq, pair), 1)
    first_head = lane < SB_HEAD_DIM
    t_pos = lax.broadcasted_iota(jnp.int32, (tq, tq), 0)
    s_pos = lax.broadcasted_iota(jnp.int32, (tq, tq), 1)
    causal = s_pos < t_pos
    q_heads = []
    for p in range(SB_PAIRS):
        q = q_ref[0, :, p * pair:(p + 1) * pair]
        zero = jnp.zeros_like(q)
        q_heads += [jnp.where(first_head, q, zero), jnp.where(first_head, zero, q)]

    def block(kb, diagonal):
        k0 = pl.multiple_of(kb * tq, tq)
        heads = range(2 * SB_PAIRS)
        kblk = [k_ref[0, pl.ds(k0, tq), p * pair:(p + 1) * pair] for p in range(SB_PAIRS)]
        vblk = [v_ref[0, pl.ds(k0, tq), p * pair:(p + 1) * pair] for p in range(SB_PAIRS)]
        z = [lax.dot_general(q_heads[h], kblk[h // 2], (((1,), (1,)), ((), ())), preferred_element_type=F32)
             for h in heads]
        halves = []
        for h in heads:
            neg_abs = lax.bitcast_convert_type(
                lax.bitcast_convert_type(z[h], jnp.uint32) | jnp.uint32(0x80000000), F32)
            sp = jnp.maximum(z[h], 0.0) + jnp.log(1.0 + jnp.exp2(neg_abs)) * LOG2E
            if diagonal:
                sp = jnp.where(causal, sp, 0.0)
            halves.append(_split(sp))
        sums = [_dot(jnp.concatenate([hi, lo], axis=1), tri_ref[...]) for hi, lo in halves]
        atts = []
        for h in heads:
            run = run_ref[h]
            att = jnp.exp2(z[h] + sums[h] + jnp.concatenate([run] * (tq // sub), axis=1))
            run_ref[h] = run + jnp.broadcast_to(sums[h][:, 0:1], run.shape)
            if diagonal:
                att = jnp.where(causal, att, 0.0)
            atts.append(att.astype(BF16))
        for h in heads:
            acc_ref[h] += _dot(atts[h], vblk[h // 2])

    acc_ref[...] = jnp.zeros_like(acc_ref)
    run_ref[...] = jnp.zeros_like(run_ref)
    block(qi, True)

    def earlier(j, carry):
        block(qi - 1 - j, False)
        return carry

    lax.fori_loop(0, qi, earlier, 0)
    for p in range(SB_PAIRS):
        cat_ref[:, p * pair:(p + 1) * pair] = jnp.where(
            first_head, acc_ref[2 * p], acc_ref[2 * p + 1]).astype(BF16)
    o_ref[0] = x_ref[0] + _dot(cat_ref[...], wo_ref[...])


def _attention(x, q, k, v, w_o):
    bsz, seq, d = q.shape
    tq, sub = SB_TILE, SB_SUB
    assert d == SB_PAIRS * 2 * SB_HEAD_DIM
    j = jnp.arange(2 * tq)[:, None] % tq
    s = jnp.arange(tq)[None, :]
    tri = -(j >= s).astype(BF16)
    tile = pl.BlockSpec((1, tq, d), lambda b, i: (b, i, 0))
    kvspec = pl.BlockSpec((1, seq, d), lambda b, i: (b, 0, 0))
    return pl.pallas_call(
        _attn_kernel,
        grid=(bsz, seq // tq),
        in_specs=[tile, tile, kvspec, kvspec, _const_spec((2 * tq, tq)), _const_spec((d, d))],
        out_specs=tile,
        out_shape=jax.ShapeDtypeStruct(x.shape, x.dtype),
        scratch_shapes=[pltpu.VMEM((2 * SB_PAIRS, tq, 2 * SB_HEAD_DIM), F32),
                        pltpu.VMEM((2 * SB_PAIRS, tq, sub), F32),
                        pltpu.VMEM((tq, d), BF16)],
        compiler_params=_cparams(),
    )(x, q, k, v, tri, w_o.astype(BF16))


def _sb_layer(x, g, w_qkv, q_g, k_g, w_o, *, ts):
    q, k, v = _qkv_proj(x, g, w_qkv, q_g, k_g, ts=ts)
    return _attention(x, q, k, v, w_o)


def _tile_rows(seq, want):
    ts = min(seq, want)
    assert seq % ts == 0 and ts % HALO == 0, (seq, ts)
    return ts


def kernel(x, norm_mix_g, norm_ffn_g, pool_w, pool_b, pool_scale, s5_lam_re, s5_lam_im, s5_log_dt, s5_b_re, s5_b_im, s5_c_re, s5_c_im, s5_d, s5_w_out, s5_b_out, lru_w_in, lru_conv_w, lru_conv_b, lru_w_a, lru_b_a, lru_w_x, lru_b_x, lru_lam, lru_w_out, sb_w_qkv, sb_q_g, sb_k_g, sb_w_o, ffn_w_in, ffn_conv_w, ffn_conv_b, ffn_w_out):
    depth = norm_mix_g.shape[0]
    seq = x.shape[1]
    normed = None
    for layer in range(depth):
        m, j = layer % 4, layer // 4
        g = norm_mix_g[layer]
        if m == 0:
            x = _pool_layer(x, g, pool_w[j], pool_b[j], pool_scale[j], ts=_tile_rows(seq, 512))
        elif m == 1:
            x = _s5_layer(x, normed, s5_lam_re[j], s5_lam_im[j], s5_log_dt[j], s5_b_re[j], s5_b_im[j],
                          s5_c_re[j], s5_c_im[j], s5_d[j], s5_w_out[j], s5_b_out[j],
                          ts=_tile_rows(seq, 512))
        elif m == 2:
            x = _lru_layer(x, g, lru_w_in[j], lru_conv_w[j], lru_conv_b[j], lru_w_a[j], lru_b_a[j],
                           lru_w_x[j], lru_b_x[j], lru_lam[j], lru_w_out[j], ts=_tile_rows(seq, 512))
        else:
            x = _sb_layer(x, g, sb_w_qkv[j], sb_q_g[j], sb_k_g[j], sb_w_o[j], ts=_tile_rows(seq, 512))
        s5_next = layer + 1 < depth and (layer + 1) % 4 == 1
        out = _conv_ffn(x, norm_ffn_g[layer], ffn_w_in[layer], ffn_conv_w[layer], ffn_conv_b[layer],
                        ffn_w_out[layer], ts=_tile_rows(seq, 1024),
                        next_gain=norm_mix_g[layer + 1] if s5_next else None)
        x, normed = out if s5_next else (out, None)
    return x
```

```python
import functools
import math

import jax
import jax.numpy as jnp
from jax import lax
from jax.experimental import pallas as pl
from jax.experimental.pallas import tpu as pltpu

F32 = jnp.float32
BF16 = jnp.bfloat16

EPS = 1e-6
LOG2E = 1.0 / math.log(2.0)
HALO = 16
POOL_WINDOWS = (2, 4, 8, 16)
S5_GROUP = 16
S5_STATE = 64
S5_SLAB = 128
LRU_BLOCKS = 4
LRU_CONV = 4
LRU_C = 8.0
SB_HEAD_DIM = 64
FFN_CONV = 3
FFN_CHUNK = 256
VMEM_LIMIT = 56 * 1024 * 1024


def _cparams():
    return pltpu.CompilerParams(
        dimension_semantics=("arbitrary", "arbitrary"),
        vmem_limit_bytes=VMEM_LIMIT)


def _rms(x, g):
    ms = jnp.mean(x * x, axis=-1, keepdims=True)
    return x * lax.rsqrt(ms + EPS) * g


def _gelu(x):
    c = math.sqrt(2.0 / math.pi)
    return 0.5 * x * (1.0 + jnp.tanh(c * (x + 0.044715 * (x * x * x))))


def _sigmoid(x):
    return 1.0 / (1.0 + jnp.exp(-x))


def _softplus(x):
    return jnp.maximum(x, 0.0) + jnp.log(1.0 + jnp.exp(-jnp.abs(x)))


def _dot(a, b):
    return jnp.dot(a, b, preferred_element_type=F32)


def _split(x):
    hi = x.astype(BF16)
    lo = (x - hi.astype(F32)).astype(BF16)
    return hi, lo


def _normed_with_halo(xh_ref, x_ref, g, xn_ref):
    first = pl.program_id(1) == 0
    hn = _rms(xh_ref[0], g)
    xn_ref[0:HALO, :] = jnp.where(first, 0.0, hn).astype(xn_ref.dtype)
    xn_ref[HALO:, :] = _rms(x_ref[0], g).astype(xn_ref.dtype)


def _tile_specs(ts, d):
    per = ts // HALO
    halo = pl.BlockSpec((1, HALO, d), lambda b, s: (b, jnp.maximum(s * per - 1, 0), 0))
    tile = pl.BlockSpec((1, ts, d), lambda b, s: (b, s, 0))
    return halo, tile


def _const_spec(shape):
    zeros = (0,) * len(shape)
    return pl.BlockSpec(shape, lambda b, s: zeros, pipeline_mode=pl.Buffered(1))


def _ffn_kernel(xh_ref, x_ref, g_ref, win_ref, cw_ref, cb_ref, wout_ref, ng_ref, o_ref, *rest, ts, hidden):
    xn_ref, a_ref = rest[-2:]
    _normed_with_halo(xh_ref, x_ref, g_ref[...], xn_ref)
    assert FFN_CONV == 3
    for c in range(hidden // FFN_CHUNK):
        conv = []
        for part in range(2):
            cols = slice(part * hidden + c * FFN_CHUNK, part * hidden + (c + 1) * FFN_CHUNK)
            h = _dot(xn_ref[...], win_ref[:, cols])
            inner = cw_ref[1:2, cols] * h + pltpu.roll(cw_ref[0:1, cols] * h, 1, 0)
            y = (cw_ref[2:3, cols] * h + cb_ref[:, cols]) + pltpu.roll(inner, 1, 0)
            conv.append(y[HALO:, :])
        val, gate = conv
        half = 0.5 * gate
        a_ref[:, c * FFN_CHUNK:(c + 1) * FFN_CHUNK] = ((half + half * jnp.tanh(half)) * val).astype(BF16)
    out = x_ref[0] + _dot(a_ref[...], wout_ref[...])
    o_ref[0] = out
    if len(rest) == 3:
        rest[0][0] = _rms(out, ng_ref[...])


def _conv_ffn(x, g, w_in, conv_w, conv_b, w_out, *, ts, next_gain=None):
    bsz, seq, d = x.shape
    hidden = w_out.shape[0]
    halo, tile = _tile_specs(ts, d)
    emit_normed = next_gain is not None
    out = jax.ShapeDtypeStruct(x.shape, x.dtype)
    return pl.pallas_call(
        functools.partial(_ffn_kernel, ts=ts, hidden=hidden),
        grid=(bsz, seq // ts),
        in_specs=[halo, tile, _const_spec((1, d)), _const_spec((d, 2 * hidden)),
                  _const_spec((FFN_CONV, 2 * hidden)), _const_spec((1, 2 * hidden)),
                  _const_spec((hidden, d)), _const_spec((1, d))],
        out_specs=[tile, tile] if emit_normed else tile,
        out_shape=[out, out] if emit_normed else out,
        scratch_shapes=[pltpu.VMEM((HALO + ts, d), BF16),
                        pltpu.VMEM((ts, hidden), BF16)],
        compiler_params=_cparams(),
    )(x, x, g.reshape(1, d), w_in.astype(BF16), conv_w, conv_b.reshape(1, -1), w_out.astype(BF16),
      (next_gain if emit_normed else g).reshape(1, d))


def _pool_kernel(xh_ref, x_ref, g_ref, w_ref, b_ref, sc_ref, o_ref, hn_ref, *, ts, group):
    _normed_with_halo(xh_ref, x_ref, g_ref[...], hn_ref)
    pos = pl.program_id(1) * ts + lax.broadcasted_iota(jnp.int32, (ts, 1), 0)
    for gi, win in enumerate(POOL_WINDOWS):
        lanes = slice(gi * group, (gi + 1) * group)
        acc = hn_ref[:, lanes]
        shift = 1
        while shift < win:
            acc = acc + pltpu.roll(acc, shift, 0)
            shift *= 2
        h = hn_ref[HALO:HALO + ts, lanes]
        count = jnp.minimum(pos + 1, win).astype(F32)
        diff = acc[HALO:, :] / count - h
        y = _dot(diff.astype(BF16), w_ref[gi]) + b_ref[:, lanes]
        o_ref[0, :, lanes] = x_ref[0, :, lanes] + sc_ref[:, lanes] * y


def _pool_layer(x, g, w, b, scale, *, ts):
    bsz, seq, d = x.shape
    ngroup, group = w.shape[0], w.shape[1]
    halo, tile = _tile_specs(ts, d)
    return pl.pallas_call(
        functools.partial(_pool_kernel, ts=ts, group=group),
        grid=(bsz, seq // ts),
        in_specs=[halo, tile, _const_spec((1, d)), _const_spec((ngroup, group, group)),
                  _const_spec((1, d)), _const_spec((1, d))],
        out_specs=tile,
        out_shape=jax.ShapeDtypeStruct(x.shape, x.dtype),
        scratch_shapes=[pltpu.VMEM((HALO + ts, d), F32)],
        compiler_params=_cparams(),
    )(x, x, g.reshape(1, d), w.astype(BF16), b.reshape(1, d), scale.reshape(1, d))


S5_CHUNK = 8
S5_LANE_CHUNK = 256


S5_SEQS = 2


def _s5_core_kernel(u_ref, wz_ref, tz_ref, wexp_ref, pw_ref, dsk_ref, o_ref, xc_ref, z_ref, y_ref, *, seq):
    m = seq // S5_CHUNK
    sl = S5_SLAB
    seqs = range(S5_SEQS)
    for b in seqs:
        for r in range(S5_CHUNK):
            xc_ref[b * m:(b + 1) * m, r * sl:(r + 1) * sl] = (
                u_ref[b, pl.ds(r, m, stride=S5_CHUNK), :].astype(BF16))
    half = z_ref.shape[2] // 2
    own = _dot(xc_ref[...], wz_ref[0])
    for b in seqs:
        z_ref[b, 0:8, :] = jnp.zeros((8, 2 * half), F32)
        z_ref[b, 8:, :] = own[b * m:(b + 1) * m]
    within = _dot(xc_ref[...], tz_ref[0])

    lc = S5_LANE_CHUNK
    for c in range(half // lc):
        re = slice(c * lc, (c + 1) * lc)
        im = slice(half + c * lc, half + (c + 1) * lc)
        pws = [(pw_ref[0, i, 0, :, re], pw_ref[0, i, 1, :, re]) for i in range(4)]

        def body(i, carry, re=re, im=im, pws=pws):
            r0 = pl.multiple_of(8 + i * 8, 8)
            out = []
            for b in seqs:
                c_re, c_im = carry[b]
                b_re = z_ref[b, pl.ds(r0, 8), re]
                b_im = z_ref[b, pl.ds(r0, 8), im]
                for step, shift in enumerate((1, 2, 4)):
                    p_re, p_im = pws[step]
                    s_re = pltpu.roll(b_re, shift, 0)
                    s_im = pltpu.roll(b_im, shift, 0)
                    b_re, b_im = (b_re + (p_re * s_re - p_im * s_im),
                                  b_im + (p_re * s_im + p_im * s_re))
                p_re, p_im = pws[3]
                h_re = b_re + (p_re * c_re - p_im * c_im)
                h_im = b_im + (p_re * c_im + p_im * c_re)
                z_ref[b, pl.ds(r0, 8), re] = h_re
                z_ref[b, pl.ds(r0, 8), im] = h_im
                out.append((jnp.broadcast_to(h_re[7:8, :], (8, lc)), jnp.broadcast_to(h_im[7:8, :], (8, lc))))
            return tuple(out)

        zero = jnp.zeros((8, lc), F32)
        lax.fori_loop(0, m // 8, body, tuple((zero, zero) for _ in seqs), unroll=True)

    before = jnp.concatenate([z_ref[b, 7:7 + m, :].astype(BF16) for b in seqs], axis=0)
    yc = within + _dot(before, wexp_ref[0])
    for b in seqs:
        for r in range(S5_CHUNK):
            rows = pl.ds(r, m, stride=S5_CHUNK)
            y_ref[b, rows, :] = _gelu(yc[b * m:(b + 1) * m, r * sl:(r + 1) * sl] + dsk_ref[...] * u_ref[b, rows, :])
    o_ref[...] = y_ref[...].astype(BF16)


def _s5_out_kernel(x_ref, y_ref, w_ref, b_ref, o_ref, *, d):
    z = _dot(y_ref[0], w_ref[...]) + b_ref[...]
    o_ref[0] = x_ref[0] + z[:, :d] * _sigmoid(z[:, d:])


def _s5_expand_kernel(a_ref, e_ref, o_ref, *, row_div, col_div, per):
    rows, cols = o_ref.shape[1:]
    full = _dot(a_ref[0], e_ref[...])
    row_group = (lax.broadcasted_iota(jnp.int32, (rows, cols), 0) >> (row_div.bit_length() - 1)) & (per - 1)
    col_group = (lax.broadcasted_iota(jnp.int32, (rows, cols), 1) >> (col_div.bit_length() - 1)) & (per - 1)
    o_ref[0] = jnp.where(row_group == col_group, full, 0.0).astype(BF16)


def _s5_expand(compact, src_of_col, *, row_div, col_div, per):
    nslab, rows, k = compact.shape
    cols = src_of_col.shape[0]
    select = (jnp.arange(k)[:, None] == src_of_col[None, :]).astype(BF16)
    return pl.pallas_call(
        functools.partial(_s5_expand_kernel, row_div=row_div, col_div=col_div, per=per),
        grid=(nslab,),
        in_specs=[pl.BlockSpec((1, rows, k), lambda j: (j, 0, 0)), pl.BlockSpec((k, cols), lambda j: (0, 0))],
        out_specs=pl.BlockSpec((1, rows, cols), lambda j: (j, 0, 0)),
        out_shape=jax.ShapeDtypeStruct((nslab, rows, cols), BF16),
        compiler_params=pltpu.CompilerParams(dimension_semantics=("arbitrary",), vmem_limit_bytes=VMEM_LIMIT),
    )(compact.astype(BF16), select)


def _s5_tables(lam_re, lam_im, log_dt, b_re, b_im, c_re, c_im, d):
    ngroups, nstate = lam_re.shape
    per = S5_SLAB // S5_GROUP
    nslab = d // S5_SLAB
    nl = S5_CHUNK
    hh = S5_GROUP
    lam = lax.complex(jnp.minimum(lam_re.astype(F32), -1e-4), lam_im.astype(F32))
    dt = jnp.exp(log_dt.astype(F32))[:, None]
    lam_bar = jnp.exp(lam * dt)
    b_bar = ((lam_bar - 1.0) / lam)[..., None] * lax.complex(b_re.astype(F32), b_im.astype(F32))
    cc = lax.complex(c_re.astype(F32), c_im.astype(F32))
    powers = [jnp.ones_like(lam_bar)]
    for _ in range(nl):
        powers.append(powers[-1] * lam_bar)
    powers = jnp.stack(powers)
    rows = nl * S5_SLAB
    col = jnp.arange(rows)

    t = powers[nl - 1::-1][:nl, :, :, None] * b_bar[None]
    t = t.reshape(nl, nslab, per, nstate, hh).transpose(1, 0, 2, 4, 3)
    wz = _s5_expand(jnp.stack([jnp.real(t), jnp.imag(t)], axis=4).reshape(nslab, rows, 2 * nstate),
                    (col // (per * nstate)) * nstate + col % nstate, row_div=hh, col_div=nstate, per=per)

    kern = jnp.real(jnp.einsum('gip,kgp,gpj->gkij', cc, powers[:nl], b_bar))
    lag = jnp.arange(nl)[None, :] - jnp.arange(nl)[:, None]
    kl = jnp.where((lag >= 0)[None, :, :, None, None], kern[:, jnp.clip(lag, 0, nl - 1)], 0.0)
    kl = kl.reshape(nslab, per, nl, nl, hh, hh).transpose(0, 2, 1, 5, 3, 4)
    step_of_col = (col // S5_SLAB) * hh + col % hh
    tz = _s5_expand(kl.reshape(nslab, rows, nl * hh), step_of_col, row_div=hh, col_div=hh, per=per)

    e = cc[None] * powers[1:nl + 1, :, None, :]
    e = e.reshape(nl, nslab, per, hh, nstate).transpose(1, 2, 4, 0, 3)
    wexp = _s5_expand(jnp.stack([jnp.real(e), -jnp.imag(e)], axis=1).reshape(nslab, 2 * per * nstate, nl * hh),
                      step_of_col, row_div=nstate, col_div=hh, per=per)

    step1 = powers[nl].reshape(nslab, 1, per * nstate)
    step2 = step1 * step1
    step4 = step2 * step2
    row = jnp.arange(8)[None, :, None]
    tabs = [jnp.where(row >= s, p, 0.0) for s, p in ((1, step1), (2, step2), (4, step4))]
    ramp = [step1]
    for _ in range(7):
        ramp.append(ramp[-1] * step1)
    tabs.append(jnp.concatenate(ramp, axis=1))
    pw = jnp.stack(tabs, axis=1)
    pw = jnp.stack([jnp.real(pw), jnp.imag(pw)], axis=2).astype(F32)
    return wz, tz, wexp, pw


def _s5_layer(x, u, lam_re, lam_im, log_dt, b_re, b_im, c_re, c_im, d_skip, w_out, b_out, *, ts):
    bsz, seq, d = x.shape
    assert seq % (8 * S5_CHUNK) == 0 and bsz % S5_SEQS == 0, (bsz, seq)
    wz, tz, wexp, pw = _s5_tables(lam_re, lam_im, log_dt, b_re, b_im, c_re, c_im, d)
    nslab = d // S5_SLAB
    tile = pl.BlockSpec((1, ts, d), lambda b, s: (b, s, 0))
    slab = pl.BlockSpec((S5_SEQS, seq, S5_SLAB), lambda j, b: (b, 0, j))
    m = seq // S5_CHUNK
    cols = pw.shape[-1]

    def per_slab(shape):
        zeros = (0,) * (len(shape) - 1)
        return pl.BlockSpec((1,) + shape[1:], lambda j, b: (j,) + zeros, pipeline_mode=pl.Buffered(1))

    y = pl.pallas_call(
        functools.partial(_s5_core_kernel, seq=seq),
        grid=(nslab, bsz // S5_SEQS),
        in_specs=[slab, per_slab(wz.shape), per_slab(tz.shape), per_slab(wexp.shape), per_slab(pw.shape),
                  pl.BlockSpec((1, S5_SLAB), lambda j, b: (0, j))],
        out_specs=slab,
        out_shape=jax.ShapeDtypeStruct(x.shape, BF16),
        scratch_shapes=[pltpu.VMEM((S5_SEQS * m, S5_CHUNK * S5_SLAB), BF16),
                        pltpu.VMEM((S5_SEQS, 8 + m, 2 * cols), F32),
                        pltpu.VMEM((S5_SEQS, seq, S5_SLAB), F32)],
        compiler_params=_cparams(),
    )(u, wz, tz, wexp, pw, d_skip.reshape(1, d))

    return pl.pallas_call(
        functools.partial(_s5_out_kernel, d=d),
        grid=(bsz, seq // ts),
        in_specs=[tile, tile, _const_spec((d, 2 * d)), _const_spec((1, 2 * d))],
        out_specs=tile,
        out_shape=jax.ShapeDtypeStruct(x.shape, x.dtype),
        compiler_params=_cparams(),
    )(x, y, w_out.astype(BF16), b_out.reshape(1, 2 * d))


def _lru_kernel(xh_ref, x_ref, g_ref, win_ref, cw_ref, cb_ref, wa_ref, ba_ref, wx_ref, bx_ref, lam_ref,
                wout_ref, o_ref, xn_ref, gb_ref, rec_ref, a_ref, b_ref, carry_ref, *, ts, width):
    _normed_with_halo(xh_ref, x_ref, g_ref[...], xn_ref)
    gb_ref[...] = _dot(xn_ref[HALO:, :], win_ref[:, :width])
    rec_ref[...] = _dot(xn_ref[...], win_ref[:, width:])
    block = width // LRU_BLOCKS
    recs, gates = [], []
    for n in range(LRU_BLOCKS):
        lanes = slice(n * block, (n + 1) * block)
        h = rec_ref[:, lanes]
        acc = cw_ref[0:1, lanes] * h
        for k in range(1, LRU_CONV):
            acc = cw_ref[k:k + 1, lanes] * h + pltpu.roll(acc, 1, 0)
        recs.append(acc[HALO:, :] + cb_ref[:, lanes])
    for n in range(LRU_BLOCKS):
        rb = recs[n].astype(BF16)
        gates.append((_dot(rb, wa_ref[n]), _dot(rb, wx_ref[n])))
    for n in range(LRU_BLOCKS):
        lanes = slice(n * block, (n + 1) * block)
        rec = recs[n]
        r = _sigmoid(gates[n][0] + ba_ref[:, lanes])
        i = _sigmoid(gates[n][1] + bx_ref[:, lanes])
        log_a = (-LRU_C * r) * _softplus(-lam_ref[:, lanes])
        a_ref[:, lanes] = jnp.exp(log_a)
        th = jnp.tanh(log_a)
        b_ref[:, lanes] = jnp.sqrt(-2.0 * th / (1.0 - th)) * (i * rec)

    @pl.when(pl.program_id(1) == 0)
    def _():
        carry_ref[...] = jnp.zeros_like(carry_ref)

    row = lax.broadcasted_iota(jnp.int32, (8, width), 0)

    def body(i, carry):
        r0 = pl.multiple_of(i * 8, 8)
        a = a_ref[pl.ds(r0, 8), :]
        b = b_ref[pl.ds(r0, 8), :]
        for shift in (1, 2, 4):
            has_partner = row >= shift
            a_prev = jnp.where(has_partner, pltpu.roll(a, shift, 0), 1.0)
            b_prev = jnp.where(has_partner, pltpu.roll(b, shift, 0), 0.0)
            b = a * b_prev + b
            a = a * a_prev
        h = a * carry + b
        b_ref[pl.ds(r0, 8), :] = h
        return jnp.broadcast_to(h[7:8, :], (8, width))

    carry_ref[...] = lax.fori_loop(0, ts // 8, body, carry_ref[...], unroll=2)
    y = (_gelu(gb_ref[...]) * b_ref[...]).astype(BF16)
    o_ref[0] = x_ref[0] + _dot(y, wout_ref[...])


def _lru_layer(x, g, w_in, conv_w, conv_b, w_a, b_a, w_x, b_x, lam, w_out, *, ts):
    bsz, seq, d = x.shape
    width = w_out.shape[0]
    halo, tile = _tile_specs(ts, d)
    row = lambda v: v.reshape(1, -1)
    return pl.pallas_call(
        functools.partial(_lru_kernel, ts=ts, width=width),
        grid=(bsz, seq // ts),
        in_specs=[halo, tile, _const_spec((1, d)), _const_spec((d, 2 * width)),
                  _const_spec((LRU_CONV, width)), _const_spec((1, width)),
                  _const_spec(w_a.shape), _const_spec((1, width)),
                  _const_spec(w_x.shape), _const_spec((1, width)), _const_spec((1, width)),
                  _const_spec((width, d))],
        out_specs=tile,
        out_shape=jax.ShapeDtypeStruct(x.shape, x.dtype),
        scratch_shapes=[pltpu.VMEM((HALO + ts, d), BF16),
                        pltpu.VMEM((ts, width), F32),
                        pltpu.VMEM((HALO + ts, width), F32),
                        pltpu.VMEM((ts, width), F32), pltpu.VMEM((ts, width), F32),
                        pltpu.VMEM((8, width), F32)],
        compiler_params=_cparams(),
    )(x, x, row(g), w_in.astype(BF16), conv_w, row(conv_b), w_a.astype(BF16), row(b_a),
      w_x.astype(BF16), row(b_x), row(lam), w_out.astype(BF16))


def _qkv_kernel(x_ref, g_ref, w_ref, qg_ref, kg_ref, q_ref, k_ref, v_ref, *, d):
    xn = _rms(x_ref[0], g_ref[...]).astype(BF16)
    pair = 2 * SB_HEAD_DIM
    first_head = lax.broadcasted_iota(jnp.int32, (x_ref.shape[1], pair), 1) < SB_HEAD_DIM

    def head_normed(t, gain_ref, out_ref, scale):
        for p in range(d // pair):
            lanes = slice(p * pair, (p + 1) * pair)
            tp = t[:, lanes]
            sq = tp * tp
            first = jnp.sum(jnp.where(first_head, sq, 0.0), axis=-1, keepdims=True)
            second = jnp.sum(jnp.where(first_head, 0.0, sq), axis=-1, keepdims=True)
            inv_first = lax.rsqrt(first * (1.0 / SB_HEAD_DIM) + EPS)
            inv_second = lax.rsqrt(second * (1.0 / SB_HEAD_DIM) + EPS)
            inv = jnp.where(first_head, inv_first, inv_second)
            out_ref[0, :, lanes] = (tp * inv * (gain_ref[:, lanes] * scale)).astype(BF16)

    head_normed(_dot(xn, w_ref[:, :d]), qg_ref, q_ref, LOG2E / math.sqrt(SB_HEAD_DIM))
    head_normed(_dot(xn, w_ref[:, d:2 * d]), kg_ref, k_ref, 1.0)
    v_ref[0] = _dot(xn, w_ref[:, 2 * d:]).astype(BF16)


def _qkv_proj(x, g, w_qkv, q_g, k_g, *, ts):
    bsz, seq, d = x.shape
    heads = d // SB_HEAD_DIM
    tile = pl.BlockSpec((1, ts, d), lambda b, s: (b, s, 0))
    out = jax.ShapeDtypeStruct(x.shape, BF16)
    return pl.pallas_call(
        functools.partial(_qkv_kernel, d=d),
        grid=(bsz, seq // ts),
        in_specs=[tile, _const_spec((1, d)), _const_spec((d, 3 * d)), _const_spec((1, d)),
                  _const_spec((1, d))],
        out_specs=[tile, tile, tile],
        out_shape=[out, out, out],
        compiler_params=_cparams(),
    )(x, g.reshape(1, d), w_qkv.astype(BF16), jnp.tile(q_g, heads).reshape(1, d),
      jnp.tile(k_g, heads).reshape(1, d))


SB_TILE = 256
SB_SUB = 128
SB_PAIRS = 8


def _attn_kernel(x_ref, q_ref, k_ref, v_ref, tri_ref, wo_ref, o_ref, acc_ref, run_ref, cat_ref):
    tq, sub, pair = SB_TILE, SB_SUB, 2 * SB_HEAD_DIM
    qi = pl.program_id(1)
    lane = lax.broadcasted_iota(jnp.int32, (tq, pair), 1)
    first_head = lane < SB_HEAD_DIM
    t_pos = lax.broadcasted_iota(jnp.int32, (tq, tq), 0)
    s_pos = lax.broadcasted_iota(jnp.int32, (tq, tq), 1)
    causal = s_pos < t_pos
    q_heads = []
    for p in range(SB_PAIRS):
        q = q_ref[0, :, p * pair:(p + 1) * pair]
        zero = jnp.zeros_like(q)
        q_heads += [jnp.where(first_head, q, zero), jnp.where(first_head, zero, q)]

    def block(kb, diagonal):
        k0 = pl.multiple_of(kb * tq, tq)
        heads = range(2 * SB_PAIRS)
        kblk = [k_ref[0, pl.ds(k0, tq), p * pair:(p + 1) * pair] for p in range(SB_PAIRS)]
        vblk = [v_ref[0, pl.ds(k0, tq), p * pair:(p + 1) * pair] for p in range(SB_PAIRS)]
        z = [lax.dot_general(q_heads[h], kblk[h // 2], (((1,), (1,)), ((), ())), preferred_element_type=F32)
             for h in heads]
        halves = []
        for h in heads:
            neg_abs = lax.bitcast_convert_type(
                lax.bitcast_convert_type(z[h], jnp.uint32) | jnp.uint32(0x80000000), F32)
            sp = jnp.maximum(z[h], 0.0) + jnp.log(1.0 + jnp.exp2(neg_abs)) * LOG2E
            if diagonal:
                sp = jnp.where(causal, sp, 0.0)
            halves.append(_split(sp))
        sums = [_dot(jnp.concatenate([hi, lo], axis=1), tri_ref[...]) for hi, lo in halves]
        atts = []
        for h in heads:
            run = run_ref[h]
            att = jnp.exp2(z[h] + sums[h] + jnp.concatenate([run] * (tq // sub), axis=1))
            run_ref[h] = run + jnp.broadcast_to(sums[h][:, 0:1], run.shape)
            if diagonal:
                att = jnp.where(causal, att, 0.0)
            atts.append(att.astype(BF16))
        for h in heads:
            acc_ref[h] += _dot(atts[h], vblk[h // 2])

    acc_ref[...] = jnp.zeros_like(acc_ref)
    run_ref[...] = jnp.zeros_like(run_ref)
    block(qi, True)

    def earlier(j, carry):
        block(qi - 1 - j, False)
        return carry

    lax.fori_loop(0, qi, earlier, 0)
    for p in range(SB_PAIRS):
        cat_ref[:, p * pair:(p + 1) * pair] = jnp.where(
            first_head, acc_ref[2 * p], acc_ref[2 * p + 1]).astype(BF16)
    o_ref[0] = x_ref[0] + _dot(cat_ref[...], wo_ref[...])


def _attention(x, q, k, v, w_o):
    bsz, seq, d = q.shape
    tq, sub = SB_TILE, SB_SUB
    assert d == SB_PAIRS * 2 * SB_HEAD_DIM
    j = jnp.arange(2 * tq)[:, None] % tq
    s = jnp.arange(tq)[None, :]
    tri = -(j >= s).astype(BF16)
    tile = pl.BlockSpec((1, tq, d), lambda b, i: (b, i, 0))
    kvspec = pl.BlockSpec((1, seq, d), lambda b, i: (b, 0, 0))
    return pl.pallas_call(
        _attn_kernel,
        grid=(bsz, seq // tq),
        in_specs=[tile, tile, kvspec, kvspec, _const_spec((2 * tq, tq)), _const_spec((d, d))],
        out_specs=tile,
        out_shape=jax.ShapeDtypeStruct(x.shape, x.dtype),
        scratch_shapes=[pltpu.VMEM((2 * SB_PAIRS, tq, 2 * SB_HEAD_DIM), F32),
                        pltpu.VMEM((2 * SB_PAIRS, tq, sub), F32),
                        pltpu.VMEM((tq, d), BF16)],
        compiler_params=_cparams(),
    )(x, q, k, v, tri, w_o.astype(BF16))


def _sb_layer(x, g, w_qkv, q_g, k_g, w_o, *, ts):
    q, k, v = _qkv_proj(x, g, w_qkv, q_g, k_g, ts=ts)
    return _attention(x, q, k, v, w_o)


def _tile_rows(seq, want):
    ts = min(seq, want)
    assert seq % ts == 0 and ts % HALO == 0, (seq, ts)
    return ts


def kernel(x, norm_mix_g, norm_ffn_g, pool_w, pool_b, pool_scale, s5_lam_re, s5_lam_im, s5_log_dt, s5_b_re, s5_b_im, s5_c_re, s5_c_im, s5_d, s5_w_out, s5_b_out, lru_w_in, lru_conv_w, lru_conv_b, lru_w_a, lru_b_a, lru_w_x, lru_b_x, lru_lam, lru_w_out, sb_w_qkv, sb_q_g, sb_k_g, sb_w_o, ffn_w_in, ffn_conv_w, ffn_conv_b, ffn_w_out):
    depth = norm_mix_g.shape[0]
    seq = x.shape[1]
    normed = None
    for layer in range(depth):
        m, j = layer % 4, layer // 4
        g = norm_mix_g[layer]
        if m == 0:
            x = _pool_layer(x, g, pool_w[j], pool_b[j], pool_scale[j], ts=_tile_rows(seq, 512))
        elif m == 1:
            x = _s5_layer(x, normed, s5_lam_re[j], s5_lam_im[j], s5_log_dt[j], s5_b_re[j], s5_b_im[j],
                          s5_c_re[j], s5_c_im[j], s5_d[j], s5_w_out[j], s5_b_out[j],
                          ts=_tile_rows(seq, 512))
        elif m == 2:
            x = _lru_layer(x, g, lru_w_in[j], lru_conv_w[j], lru_conv_b[j], lru_w_a[j], lru_b_a[j],
                           lru_w_x[j], lru_b_x[j], lru_lam[j], lru_w_out[j], ts=_tile_rows(seq, 512))
        else:
            x = _sb_layer(x, g, sb_w_qkv[j], sb_q_g[j], sb_k_g[j], sb_w_o[j], ts=_tile_rows(seq, 512))
        s5_next = layer + 1 < depth and (layer + 1) % 4 == 1
        out = _conv_ffn(x, norm_ffn_g[layer], ffn_w_in[layer], ffn_conv_w[layer], ffn_conv_b[layer],
                        ffn_w_out[layer], ts=_tile_rows(seq, 1024),
                        next_gain=norm_mix_g[layer + 1] if s5_next else None)
        x, normed = out if s5_next else (out, None)
    return x
```

```python
import functools
import math

import jax
import jax.numpy as jnp
from jax import lax
from jax.experimental import pallas as pl
from jax.experimental.pallas import tpu as pltpu

F32 = jnp.float32
BF16 = jnp.bfloat16

EPS = 1e-6
LOG2E = 1.0 / math.log(2.0)
HALO = 16
POOL_WINDOWS = (2, 4, 8, 16)
S5_GROUP = 16
S5_STATE = 64
S5_SLAB = 128
LRU_BLOCKS = 4
LRU_CONV = 4
LRU_C = 8.0
SB_HEAD_DIM = 64
FFN_CONV = 3
FFN_CHUNK = 256
VMEM_LIMIT = 56 * 1024 * 1024


def _cparams():
    return pltpu.CompilerParams(
        dimension_semantics=("arbitrary", "arbitrary"),
        vmem_limit_bytes=VMEM_LIMIT)


def _rms(x, g):
    ms = jnp.mean(x * x, axis=-1, keepdims=True)
    return x * lax.rsqrt(ms + EPS) * g


def _gelu(x):
    c = math.sqrt(2.0 / math.pi)
    return 0.5 * x * (1.0 + jnp.tanh(c * (x + 0.044715 * (x * x * x))))


def _sigmoid(x):
    return 1.0 / (1.0 + jnp.exp(-x))


def _softplus(x):
    return jnp.maximum(x, 0.0) + jnp.log(1.0 + jnp.exp(-jnp.abs(x)))


def _dot(a, b):
    return jnp.dot(a, b, preferred_element_type=F32)


def _split(x):
    hi = x.astype(BF16)
    lo = (x - hi.astype(F32)).astype(BF16)
    return hi, lo


def _normed_with_halo(xh_ref, x_ref, g, xn_ref):
    first = pl.program_id(1) == 0
    hn = _rms(xh_ref[0], g)
    xn_ref[0:HALO, :] = jnp.where(first, 0.0, hn).astype(xn_ref.dtype)
    xn_ref[HALO:, :] = _rms(x_ref[0], g).astype(xn_ref.dtype)


def _tile_specs(ts, d):
    per = ts // HALO
    halo = pl.BlockSpec((1, HALO, d), lambda b, s: (b, jnp.maximum(s * per - 1, 0), 0))
    tile = pl.BlockSpec((1, ts, d), lambda b, s: (b, s, 0))
    return halo, tile


def _const_spec(shape):
    zeros = (0,) * len(shape)
    return pl.BlockSpec(shape, lambda b, s: zeros, pipeline_mode=pl.Buffered(1))


def _ffn_kernel(xh_ref, x_ref, g_ref, win_ref, cw_ref, cb_ref, wout_ref, ng_ref, o_ref, *rest, ts, hidden):
    xn_ref, a_ref = rest[-2:]
    _normed_with_halo(xh_ref, x_ref, g_ref[...], xn_ref)
    assert FFN_CONV == 3
    for c in range(hidden // FFN_CHUNK):
        conv = []
        for part in range(2):
            cols = slice(part * hidden + c * FFN_CHUNK, part * hidden + (c + 1) * FFN_CHUNK)
            h = _dot(xn_ref[...], win_ref[:, cols])
            inner = cw_ref[1:2, cols] * h + pltpu.roll(cw_ref[0:1, cols] * h, 1, 0)
            y = (cw_ref[2:3, cols] * h + cb_ref[:, cols]) + pltpu.roll(inner, 1, 0)
            conv.append(y[HALO:, :])
        val, gate = conv
        half = 0.5 * gate
        a_ref[:, c * FFN_CHUNK:(c + 1) * FFN_CHUNK] = ((half + half * jnp.tanh(half)) * val).astype(BF16)
    out = x_ref[0] + _dot(a_ref[...], wout_ref[...])
    o_ref[0] = out
    if len(rest) == 3:
        rest[0][0] = _rms(out, ng_ref[...])


def _conv_ffn(x, g, w_in, conv_w, conv_b, w_out, *, ts, next_gain=None):
    bsz, seq, d = x.shape
    hidden = w_out.shape[0]
    halo, tile = _tile_specs(ts, d)
    emit_normed = next_gain is not None
    out = jax.ShapeDtypeStruct(x.shape, x.dtype)
    return pl.pallas_call(
        functools.partial(_ffn_kernel, ts=ts, hidden=hidden),
        grid=(bsz, seq // ts),
        in_specs=[halo, tile, _const_spec((1, d)), _const_spec((d, 2 * hidden)),
                  _const_spec((FFN_CONV, 2 * hidden)), _const_spec((1, 2 * hidden)),
                  _const_spec((hidden, d)), _const_spec((1, d))],
        out_specs=[tile, tile] if emit_normed else tile,
        out_shape=[out, out] if emit_normed else out,
        scratch_shapes=[pltpu.VMEM((HALO + ts, d), BF16),
                        pltpu.VMEM((ts, hidden), BF16)],
        compiler_params=_cparams(),
    )(x, x, g.reshape(1, d), w_in.astype(BF16), conv_w, conv_b.reshape(1, -1), w_out.astype(BF16),
      (next_gain if emit_normed else g).reshape(1, d))


def _pool_kernel(xh_ref, x_ref, g_ref, w_ref, b_ref, sc_ref, o_ref, hn_ref, *, ts, group):
    _normed_with_halo(xh_ref, x_ref, g_ref[...], hn_ref)
    pos = pl.program_id(1) * ts + lax.broadcasted_iota(jnp.int32, (ts, 1), 0)
    for gi, win in enumerate(POOL_WINDOWS):
        lanes = slice(gi * group, (gi + 1) * group)
        acc = hn_ref[:, lanes]
        shift = 1
        while shift < win:
            acc = acc + pltpu.roll(acc, shift, 0)
            shift *= 2
        h = hn_ref[HALO:HALO + ts, lanes]
        count = jnp.minimum(pos + 1, win).astype(F32)
        diff = acc[HALO:, :] / count - h
        y = _dot(diff.astype(BF16), w_ref[gi]) + b_ref[:, lanes]
        o_ref[0, :, lanes] = x_ref[0, :, lanes] + sc_ref[:, lanes] * y


def _pool_layer(x, g, w, b, scale, *, ts):
    bsz, seq, d = x.shape
    ngroup, group = w.shape[0], w.shape[1]
    halo, tile = _tile_specs(ts, d)
    return pl.pallas_call(
        functools.partial(_pool_kernel, ts=ts, group=group),
        grid=(bsz, seq // ts),
        in_specs=[halo, tile, _const_spec((1, d)), _const_spec((ngroup, group, group)),
                  _const_spec((1, d)), _const_spec((1, d))],
        out_specs=tile,
        out_shape=jax.ShapeDtypeStruct(x.shape, x.dtype),
        scratch_shapes=[pltpu.VMEM((HALO + ts, d), F32)],
        compiler_params=_cparams(),
    )(x, x, g.reshape(1, d), w.astype(BF16), b.reshape(1, d), scale.reshape(1, d))


S5_CHUNK = 8
S5_LANE_CHUNK = 256


S5_SEQS = 2


def _s5_core_kernel(u_ref, wz_ref, tz_ref, wexp_ref, pw_ref, dsk_ref, o_ref, xc_ref, z_ref, y_ref, *, seq):
    m = seq // S5_CHUNK
    sl = S5_SLAB
    seqs = range(S5_SEQS)
    for b in seqs:
        for r in range(S5_CHUNK):
            xc_ref[b * m:(b + 1) * m, r * sl:(r + 1) * sl] = (
                u_ref[b, pl.ds(r, m, stride=S5_CHUNK), :].astype(BF16))
    half = z_ref.shape[2] // 2
    own = _dot(xc_ref[...], wz_ref[0])
    for b in seqs:
        z_ref[b, 0:8, :] = jnp.zeros((8, 2 * half), F32)
        z_ref[b, 8:, :] = own[b * m:(b + 1) * m]
    within = _dot(xc_ref[...], tz_ref[0])

    lc = S5_LANE_CHUNK
    for c in range(half // lc):
        re = slice(c * lc, (c + 1) * lc)
        im = slice(half + c * lc, half + (c + 1) * lc)
        pws = [(pw_ref[0, i, 0, :, re], pw_ref[0, i, 1, :, re]) for i in range(4)]

        def body(i, carry, re=re, im=im, pws=pws):
            r0 = pl.multiple_of(8 + i * 8, 8)
            out = []
            for b in seqs:
                c_re, c_im = carry[b]
                b_re = z_ref[b, pl.ds(r0, 8), re]
                b_im = z_ref[b, pl.ds(r0, 8), im]
                for step, shift in enumerate((1, 2, 4)):
                    p_re, p_im = pws[step]
                    s_re = pltpu.roll(b_re, shift, 0)
                    s_im = pltpu.roll(b_im, shift, 0)
                    b_re, b_im = (b_re + (p_re * s_re - p_im * s_im),
                                  b_im + (p_re * s_im + p_im * s_re))
                p_re, p_im = pws[3]
                h_re = b_re + (p_re * c_re - p_im * c_im)
                h_im = b_im + (p_re * c_im + p_im * c_re)
                z_ref[b, pl.ds(r0, 8), re] = h_re
                z_ref[b, pl.ds(r0, 8), im] = h_im
                out.append((jnp.broadcast_to(h_re[7:8, :], (8, lc)), jnp.broadcast_to(h_im[7:8, :], (8, lc))))
            return tuple(out)

        zero = jnp.zeros((8, lc), F32)
        lax.fori_loop(0, m // 8, body, tuple((zero, zero) for _ in seqs), unroll=True)

    before = jnp.concatenate([z_ref[b, 7:7 + m, :].astype(BF16) for b in seqs], axis=0)
    yc = within + _dot(before, wexp_ref[0])
    for b in seqs:
        for r in range(S5_CHUNK):
            rows = pl.ds(r, m, stride=S5_CHUNK)
            y_ref[b, rows, :] = _gelu(yc[b * m:(b + 1) * m, r * sl:(r + 1) * sl] + dsk_ref[...] * u_ref[b, rows, :])
    o_ref[...] = y_ref[...].astype(BF16)


def _s5_out_kernel(x_ref, y_ref, w_ref, b_ref, o_ref, *, d):
    z = _dot(y_ref[0], w_ref[...]) + b_ref[...]
    o_ref[0] = x_ref[0] + z[:, :d] * _sigmoid(z[:, d:])


def _s5_expand_kernel(a_ref, e_ref, o_ref, *, row_div, col_div, per):
    rows, cols = o_ref.shape[1:]
    full = _dot(a_ref[0], e_ref[...])
    row_group = (lax.broadcasted_iota(jnp.int32, (rows, cols), 0) >> (row_div.bit_length() - 1)) & (per - 1)
    col_group = (lax.broadcasted_iota(jnp.int32, (rows, cols), 1) >> (col_div.bit_length() - 1)) & (per - 1)
    o_ref[0] = jnp.where(row_group == col_group, full, 0.0).astype(BF16)


def _s5_expand(compact, src_of_col, *, row_div, col_div, per):
    nslab, rows, k = compact.shape
    cols = src_of_col.shape[0]
    select = (jnp.arange(k)[:, None] == src_of_col[None, :]).astype(BF16)
    return pl.pallas_call(
        functools.partial(_s5_expand_kernel, row_div=row_div, col_div=col_div, per=per),
        grid=(nslab,),
        in_specs=[pl.BlockSpec((1, rows, k), lambda j: (j, 0, 0)), pl.BlockSpec((k, cols), lambda j: (0, 0))],
        out_specs=pl.BlockSpec((1, rows, cols), lambda j: (j, 0, 0)),
        out_shape=jax.ShapeDtypeStruct((nslab, rows, cols), BF16),
        compiler_params=pltpu.CompilerParams(dimension_semantics=("arbitrary",), vmem_limit_bytes=VMEM_LIMIT),
    )(compact.astype(BF16), select)


def _s5_tables(lam_re, lam_im, log_dt, b_re, b_im, c_re, c_im, d):
    ngroups, nstate = lam_re.shape
    per = S5_SLAB // S5_GROUP
    nslab = d // S5_SLAB
    nl = S5_CHUNK
    hh = S5_GROUP
    lam = lax.complex(jnp.minimum(lam_re.astype(F32), -1e-4), lam_im.astype(F32))
    dt = jnp.exp(log_dt.astype(F32))[:, None]
    lam_bar = jnp.exp(lam * dt)
    b_bar = ((lam_bar - 1.0) / lam)[..., None] * lax.complex(b_re.astype(F32), b_im.astype(F32))
    cc = lax.complex(c_re.astype(F32), c_im.astype(F32))
    powers = [jnp.ones_like(lam_bar)]
    for _ in range(nl):
        powers.append(powers[-1] * lam_bar)
    powers = jnp.stack(powers)
    rows = nl * S5_SLAB
    col = jnp.arange(rows)

    t = powers[nl - 1::-1][:nl, :, :, None] * b_bar[None]
    t = t.reshape(nl, nslab, per, nstate, hh).transpose(1, 0, 2, 4, 3)
    wz = _s5_expand(jnp.stack([jnp.real(t), jnp.imag(t)], axis=4).reshape(nslab, rows, 2 * nstate),
                    (col // (per * nstate)) * nstate + col % nstate, row_div=hh, col_div=nstate, per=per)

    kern = jnp.real(jnp.einsum('gip,kgp,gpj->gkij', cc, powers[:nl], b_bar))
    lag = jnp.arange(nl)[None, :] - jnp.arange(nl)[:, None]
    kl = jnp.where((lag >= 0)[None, :, :, None, None], kern[:, jnp.clip(lag, 0, nl - 1)], 0.0)
    kl = kl.reshape(nslab, per, nl, nl, hh, hh).transpose(0, 2, 1, 5, 3, 4)
    step_of_col = (col // S5_SLAB) * hh + col % hh
    tz = _s5_expand(kl.reshape(nslab, rows, nl * hh), step_of_col, row_div=hh, col_div=hh, per=per)

    e = cc[None] * powers[1:nl + 1, :, None, :]
    e = e.reshape(nl, nslab, per, hh, nstate).transpose(1, 2, 4, 0, 3)
    wexp = _s5_expand(jnp.stack([jnp.real(e), -jnp.imag(e)], axis=1).reshape(nslab, 2 * per * nstate, nl * hh),
                      step_of_col, row_div=nstate, col_div=hh, per=per)

    step1 = powers[nl].reshape(nslab, 1, per * nstate)
    step2 = step1 * step1
    step4 = step2 * step2
    row = jnp.arange(8)[None, :, None]
    tabs = [jnp.where(row >= s, p, 0.0) for s, p in ((1, step1), (2, step2), (4, step4))]
    ramp = [step1]
    for _ in range(7):
        ramp.append(ramp[-1] * step1)
    tabs.append(jnp.concatenate(ramp, axis=1))
    pw = jnp.stack(tabs, axis=1)
    pw = jnp.stack([jnp.real(pw), jnp.imag(pw)], axis=2).astype(F32)
    return wz, tz, wexp, pw


def _s5_layer(x, u, lam_re, lam_im, log_dt, b_re, b_im, c_re, c_im, d_skip, w_out, b_out, *, ts):
    bsz, seq, d = x.shape
    assert seq % (8 * S5_CHUNK) == 0 and bsz % S5_SEQS == 0, (bsz, seq)
    wz, tz, wexp, pw = _s5_tables(lam_re, lam_im, log_dt, b_re, b_im, c_re, c_im, d)
    nslab = d // S5_SLAB
    tile = pl.BlockSpec((1, ts, d), lambda b, s: (b, s, 0))
    slab = pl.BlockSpec((S5_SEQS, seq, S5_SLAB), lambda j, b: (b, 0, j))
    m = seq // S5_CHUNK
    cols = pw.shape[-1]

    def per_slab(shape):
        zeros = (0,) * (len(shape) - 1)
        return pl.BlockSpec((1,) + shape[1:], lambda j, b: (j,) + zeros, pipeline_mode=pl.Buffered(1))

    y = pl.pallas_call(
        functools.partial(_s5_core_kernel, seq=seq),
        grid=(nslab, bsz // S5_SEQS),
        in_specs=[slab, per_slab(wz.shape), per_slab(tz.shape), per_slab(wexp.shape), per_slab(pw.shape),
                  pl.BlockSpec((1, S5_SLAB), lambda j, b: (0, j))],
        out_specs=slab,
        out_shape=jax.ShapeDtypeStruct(x.shape, BF16),
        scratch_shapes=[pltpu.VMEM((S5_SEQS * m, S5_CHUNK * S5_SLAB), BF16),
                        pltpu.VMEM((S5_SEQS, 8 + m, 2 * cols), F32),
                        pltpu.VMEM((S5_SEQS, seq, S5_SLAB), F32)],
        compiler_params=_cparams(),
    )(u, wz, tz, wexp, pw, d_skip.reshape(1, d))

    return pl.pallas_call(
        functools.partial(_s5_out_kernel, d=d),
        grid=(bsz, seq // ts),
        in_specs=[tile, tile, _const_spec((d, 2 * d)), _const_spec((1, 2 * d))],
        out_specs=tile,
        out_shape=jax.ShapeDtypeStruct(x.shape, x.dtype),
        compiler_params=_cparams(),
    )(x, y, w_out.astype(BF16), b_out.reshape(1, 2 * d))


def _lru_kernel(xh_ref, x_ref, g_ref, win_ref, cw_ref, cb_ref, wa_ref, ba_ref, wx_ref, bx_ref, lam_ref,
                wout_ref, o_ref, xn_ref, gb_ref, rec_ref, a_ref, b_ref, carry_ref, *, ts, width):
    _normed_with_halo(xh_ref, x_ref, g_ref[...], xn_ref)
    gb_ref[...] = _dot(xn_ref[HALO:, :], win_ref[:, :width])
    rec_ref[...] = _dot(xn_ref[...], win_ref[:, width:])
    block = width // LRU_BLOCKS
    recs, gates = [], []
    for n in range(LRU_BLOCKS):
        lanes = slice(n * block, (n + 1) * block)
        h = rec_ref[:, lanes]
        acc = cw_ref[0:1, lanes] * h
        for k in range(1, LRU_CONV):
            acc = cw_ref[k:k + 1, lanes] * h + pltpu.roll(acc, 1, 0)
        recs.append(acc[HALO:, :] + cb_ref[:, lanes])
    for n in range(LRU_BLOCKS):
        rb = recs[n].astype(BF16)
        gates.append((_dot(rb, wa_ref[n]), _dot(rb, wx_ref[n])))
    for n in range(LRU_BLOCKS):
        lanes = slice(n * block, (n + 1) * block)
        rec = recs[n]
        r = _sigmoid(gates[n][0] + ba_ref[:, lanes])
        i = _sigmoid(gates[n][1] + bx_ref[:, lanes])
        log_a = (-LRU_C * r) * _softplus(-lam_ref[:, lanes])
        a_ref[:, lanes] = jnp.exp(log_a)
        th = jnp.tanh(log_a)
        b_ref[:, lanes] = jnp.sqrt(-2.0 * th / (1.0 - th)) * (i * rec)

    @pl.when(pl.program_id(1) == 0)
    def _():
        carry_ref[...] = jnp.zeros_like(carry_ref)

    row = lax.broadcasted_iota(jnp.int32, (8, width), 0)

    def body(i, carry):
        r0 = pl.multiple_of(i * 8, 8)
        a = a_ref[pl.ds(r0, 8), :]
        b = b_ref[pl.ds(r0, 8), :]
        for shift in (1, 2, 4):
            has_partner = row >= shift
            a_prev = jnp.where(has_partner, pltpu.roll(a, shift, 0), 1.0)
            b_prev = jnp.where(has_partner, pltpu.roll(b, shift, 0), 0.0)
            b = a * b_prev + b
            a = a * a_prev
        h = a * carry + b
        b_ref[pl.ds(r0, 8), :] = h
        return jnp.broadcast_to(h[7:8, :], (8, width))

    carry_ref[...] = lax.fori_loop(0, ts // 8, body, carry_ref[...], unroll=2)
    y = (_gelu(gb_ref[...]) * b_ref[...]).astype(BF16)
    o_ref[0] = x_ref[0] + _dot(y, wout_ref[...])


def _lru_layer(x, g, w_in, conv_w, conv_b, w_a, b_a, w_x, b_x, lam, w_out, *, ts):
    bsz, seq, d = x.shape
    width = w_out.shape[0]
    halo, tile = _tile_specs(ts, d)
    row = lambda v: v.reshape(1, -1)
    return pl.pallas_call(
        functools.partial(_lru_kernel, ts=ts, width=width),
        grid=(bsz, seq // ts),
        in_specs=[halo, tile, _const_spec((1, d)), _const_spec((d, 2 * width)),
                  _const_spec((LRU_CONV, width)), _const_spec((1, width)),
                  _const_spec(w_a.shape), _const_spec((1, width)),
                  _const_spec(w_x.shape), _const_spec((1, width)), _const_spec((1, width)),
                  _const_spec((width, d))],
        out_specs=tile,
        out_shape=jax.ShapeDtypeStruct(x.shape, x.dtype),
        scratch_shapes=[pltpu.VMEM((HALO + ts, d), BF16),
                        pltpu.VMEM((ts, width), F32),
                        pltpu.VMEM((HALO + ts, width), F32),
                        pltpu.VMEM((ts, width), F32), pltpu.VMEM((ts, width), F32),
                        pltpu.VMEM((8, width), F32)],
        compiler_params=_cparams(),
    )(x, x, row(g), w_in.astype(BF16), conv_w, row(conv_b), w_a.astype(BF16), row(b_a),
      w_x.astype(BF16), row(b_x), row(lam), w_out.astype(BF16))


def _qkv_kernel(x_ref, g_ref, w_ref, qg_ref, kg_ref, q_ref, k_ref, v_ref, *, d):
    xn = _rms(x_ref[0], g_ref[...]).astype(BF16)
    pair = 2 * SB_HEAD_DIM
    first_head = lax.broadcasted_iota(jnp.int32, (x_ref.shape[1], pair), 1) < SB_HEAD_DIM

    def head_normed(t, gain_ref, out_ref, scale):
        for p in range(d // pair):
            lanes = slice(p * pair, (p + 1) * pair)
            tp = t[:, lanes]
            sq = tp * tp
            first = jnp.sum(jnp.where(first_head, sq, 0.0), axis=-1, keepdims=True)
            second = jnp.sum(jnp.where(first_head, 0.0, sq), axis=-1, keepdims=True)
            inv_first = lax.rsqrt(first * (1.0 / SB_HEAD_DIM) + EPS)
            inv_second = lax.rsqrt(second * (1.0 / SB_HEAD_DIM) + EPS)
            inv = jnp.where(first_head, inv_first, inv_second)
            out_ref[0, :, lanes] = (tp * inv * (gain_ref[:, lanes] * scale)).astype(BF16)

    head_normed(_dot(xn, w_ref[:, :d]), qg_ref, q_ref, LOG2E / math.sqrt(SB_HEAD_DIM))
    head_normed(_dot(xn, w_ref[:, d:2 * d]), kg_ref, k_ref, 1.0)
    v_ref[0] = _dot(xn, w_ref[:, 2 * d:]).astype(BF16)


def _qkv_proj(x, g, w_qkv, q_g, k_g, *, ts):
    bsz, seq, d = x.shape
    heads = d // SB_HEAD_DIM
    tile = pl.BlockSpec((1, ts, d), lambda b, s: (b, s, 0))
    out = jax.ShapeDtypeStruct(x.shape, BF16)
    return pl.pallas_call(
        functools.partial(_qkv_kernel, d=d),
        grid=(bsz, seq // ts),
        in_specs=[tile, _const_spec((1, d)), _const_spec((d, 3 * d)), _const_spec((1, d)),
                  _const_spec((1, d))],
        out_specs=[tile, tile, tile],
        out_shape=[out, out, out],
        compiler_params=_cparams(),
    )(x, g.reshape(1, d), w_qkv.astype(BF16), jnp.tile(q_g, heads).reshape(1, d),
      jnp.tile(k_g, heads).reshape(1, d))


SB_TILE = 256
SB_SUB = 128
SB_PAIRS = 8


def _attn_kernel(x_ref, q_ref, k_ref, v_ref, tri_ref, wo_ref, o_ref, acc_ref, run_ref, cat_ref):
    tq, sub, pair = SB_TILE, SB_SUB, 2 * SB_HEAD_DIM
    qi = pl.program_id(1)
    lane = lax.broadcasted_iota(jnp.int32, (tq, pair), 1)
    first_head = lane < SB_HEAD_DIM
    t_pos = lax.broadcasted_iota(jnp.int32, (tq, tq), 0)
    s_pos = lax.broadcasted_iota(jnp.int32, (tq, tq), 1)
    causal = s_pos < t_pos
    q_heads = []
    for p in range(SB_PAIRS):
        q = q_ref[0, :, p * pair:(p + 1) * pair]
        zero = jnp.zeros_like(q)
        q_heads += [jnp.where(first_head, q, zero), jnp.where(first_head, zero, q)]

    def block(kb, diagonal):
        k0 = pl.multiple_of(kb * tq, tq)
        heads = range(2 * SB_PAIRS)
        kblk = [k_ref[0, pl.ds(k0, tq), p * pair:(p + 1) * pair] for p in range(SB_PAIRS)]
        vblk = [v_ref[0, pl.ds(k0, tq), p * pair:(p + 1) * pair] for p in range(SB_PAIRS)]
        z = [lax.dot_general(q_heads[h], kblk[h // 2], (((1,), (1,)), ((), ())), preferred_element_type=F32)
             for h in heads]
        halves = []
        for h in heads:
            neg_abs = lax.bitcast_convert_type(
                lax.bitcast_convert_type(z[h], jnp.uint32) | jnp.uint32(0x80000000), F32)
            sp = jnp.maximum(z[h], 0.0) + jnp.log(1.0 + jnp.exp2(neg_abs)) * LOG2E
            if diagonal:
                sp = jnp.where(causal, sp, 0.0)
            halves.append(_split(sp))
        sums = [_dot(jnp.concatenate([hi, lo], axis=1), tri_ref[...]) for hi, lo in halves]
        atts = []
        for h in heads:
            run = run_ref[h]
            att = jnp.exp2(z[h] + sums[h] + jnp.concatenate([run] * (tq // sub), axis=1))
            run_ref[h] = run + jnp.broadcast_to(sums[h][:, 0:1], run.shape)
            if diagonal:
                att = jnp.where(causal, att, 0.0)
            atts.append(att.astype(BF16))
        for h in heads:
            acc_ref[h] += _dot(atts[h], vblk[h // 2])

    acc_ref[...] = jnp.zeros_like(acc_ref)
    run_ref[...] = jnp.zeros_like(run_ref)
    block(qi, True)

    def earlier(j, carry):
        block(qi - 1 - j, False)
        return carry

    lax.fori_loop(0, qi, earlier, 0)
    for p in range(SB_PAIRS):
        cat_ref[:, p * pair:(p + 1) * pair] = jnp.where(
            first_head, acc_ref[2 * p], acc_ref[2 * p + 1]).astype(BF16)
    o_ref[0] = x_ref[0] + _dot(cat_ref[...], wo_ref[...])


def _attention(x, q, k, v, w_o):
    bsz, seq, d = q.shape
    tq, sub = SB_TILE, SB_SUB
    assert d == SB_PAIRS * 2 * SB_HEAD_DIM
    j = jnp.arange(2 * tq)[:, None] % tq
    s = jnp.arange(tq)[None, :]
    tri = -(j >= s).astype(BF16)
    tile = pl.BlockSpec((1, tq, d), lambda b, i: (b, i, 0))
    kvspec = pl.BlockSpec((1, seq, d), lambda b, i: (b, 0, 0))
    return pl.pallas_call(
        _attn_kernel,
        grid=(bsz, seq // tq),
        in_specs=[tile, tile, kvspec, kvspec, _const_spec((2 * tq, tq)), _const_spec((d, d))],
        out_specs=tile,
        out_shape=jax.ShapeDtypeStruct(x.shape, x.dtype),
        scratch_shapes=[pltpu.VMEM((2 * SB_PAIRS, tq, 2 * SB_HEAD_DIM), F32),
                        pltpu.VMEM((2 * SB_PAIRS, tq, sub), F32),
                        pltpu.VMEM((tq, d), BF16)],
        compiler_params=_cparams(),
    )(x, q, k, v, tri, w_o.astype(BF16))


def _sb_layer(x, g, w_qkv, q_g, k_g, w_o, *, ts):
    q, k, v = _qkv_proj(x, g, w_qkv, q_g, k_g, ts=ts)
    return _attention(x, q, k, v, w_o)


def _tile_rows(seq, want):
    ts = min(seq, want)
    assert seq % ts == 0 and ts % HALO == 0, (seq, ts)
    return ts


def kernel(x, norm_mix_g, norm_ffn_g, pool_w, pool_b, pool_scale, s5_lam_re, s5_lam_im, s5_log_dt, s5_b_re, s5_b_im, s5_c_re, s5_c_im, s5_d, s5_w_out, s5_b_out, lru_w_in, lru_conv_w, lru_conv_b, lru_w_a, lru_b_a, lru_w_x, lru_b_x, lru_lam, lru_w_out, sb_w_qkv, sb_q_g, sb_k_g, sb_w_o, ffn_w_in, ffn_conv_w, ffn_conv_b, ffn_w_out):
    depth = norm_mix_g.shape[0]
    seq = x.shape[1]
    normed = None
    for layer in range(depth):
        m, j = layer % 4, layer // 4
        g = norm_mix_g[layer]
        if m == 0:
            x = _pool_layer(x, g, pool_w[j], pool_b[j], pool_scale[j], ts=_tile_rows(seq, 1024))
        elif m == 1:
            x = _s5_layer(x, normed, s5_lam_re[j], s5_lam_im[j], s5_log_dt[j], s5_b_re[j], s5_b_im[j],
                          s5_c_re[j], s5_c_im[j], s5_d[j], s5_w_out[j], s5_b_out[j],
                          ts=_tile_rows(seq, 1024))
        elif m == 2:
            x = _lru_layer(x, g, lru_w_in[j], lru_conv_w[j], lru_conv_b[j], lru_w_a[j], lru_b_a[j],
                           lru_w_x[j], lru_b_x[j], lru_lam[j], lru_w_out[j], ts=_tile_rows(seq, 1024))
        else:
            x = _sb_layer(x, g, sb_w_qkv[j], sb_q_g[j], sb_k_g[j], sb_w_o[j], ts=_tile_rows(seq, 1024))
        s5_next = layer + 1 < depth and (layer + 1) % 4 == 1
        out = _conv_ffn(x, norm_ffn_g[layer], ffn_w_in[layer], ffn_conv_w[layer], ffn_conv_b[layer],
                        ffn_w_out[layer], ts=_tile_rows(seq, 1024),
                        next_gain=norm_mix_g[layer + 1] if s5_next else None)
        x, normed = out if s5_next else (out, None)
    return x
```
